```python
import math
import jax, jax.numpy as jnp
from jax import lax
import numpy as np

D_MODEL = 1024
BATCH = 32
SEQ = 256
DEPTH = 4
DEC_BATCH = 4
DEC_SEQ = 4096
PAST_LEN = 512

GRID_W = 64
CHUNK = 128
Q_BLOCK = 128
EPS = 1e-6
ROPE_BASE = 10000.0

D_FF = 2816

SSD_HEADS = 16
SSD_HEAD_DIM = 64
SSD_D_INNER = SSD_HEADS * SSD_HEAD_DIM
SSD_GROUPS = 2
SSD_STATE = 64
SSD_CONV = 5
SSD_CONV_CH = SSD_D_INNER + 2 * SSD_GROUPS * SSD_STATE

MLA_HEADS = 8
MLA_NOPE = 128
MLA_ROPE = 64
MLA_V = 128
MLA_Q_LORA = 384
MLA_KV_LORA = 256

RET_HEADS = 8
RET_DK = 64
RET_DV = 128

N_BRANCH = 3
IN_SPLITS = (SSD_D_INNER, SSD_CONV_CH, SSD_HEADS, SSD_HEADS,
             MLA_Q_LORA, MLA_KV_LORA, MLA_ROPE,
             RET_HEADS * RET_DK, RET_HEADS * RET_DK, RET_HEADS * RET_DV, RET_HEADS * RET_DV,
             N_BRANCH * D_MODEL)
IN_COLS = sum(IN_SPLITS)

kernel_name = 'hybrid_diffusion_ssd_mla_retention_step'

F32 = jnp.float32


def split_cols(y, sizes):
    idx, acc = [], 0
    for s in sizes[:-1]:
        acc += s
        idx.append(acc)
    return jnp.split(y, idx, axis=-1)


def rmsnorm(x, w):
    xf = x.astype(F32)
    y = xf * lax.rsqrt(jnp.mean(xf * xf, axis=-1, keepdims=True) + EPS)
    return (y * w.astype(F32)).astype(x.dtype)


def head_groupnorm(o, w):
    of = o.astype(F32)
    mu = jnp.mean(of, axis=-1, keepdims=True)
    var = jnp.mean(jnp.square(of - mu), axis=-1, keepdims=True)
    y = ((of - mu) * lax.rsqrt(var + EPS)).reshape(o.shape[:2] + (-1,))
    return (y * w.astype(F32)).astype(o.dtype)


def axial_rope(x):
    n, r = x.shape[1], x.shape[-1]
    nf = r // 4
    t = jnp.arange(n)
    pos = jnp.stack([t // GRID_W, t % GRID_W], axis=-1).astype(F32)
    inv = ROPE_BASE ** (-jnp.arange(nf, dtype=F32) / nf)
    ang = (pos[:, :, None] * inv).reshape((n,) + (1,) * (x.ndim - 3) + (2, nf))
    cos, sin = jnp.cos(ang), jnp.sin(ang)
    xr = x.astype(F32).reshape(x.shape[:-1] + (2, 2, nf))
    x1, x2 = xr[..., 0, :], xr[..., 1, :]
    out = jnp.stack([x1 * cos - x2 * sin, x1 * sin + x2 * cos], axis=-2)
    return out.reshape(x.shape).astype(x.dtype)


def depthwise_conv(u, w, b):
    out = lax.conv_general_dilated(u, w[:, None, :], window_strides=(1,), padding='SAME',
                                   dimension_numbers=('NWC', 'WIO', 'NWC'),
                                   feature_group_count=u.shape[-1])
    return out + b


def swiglu(u, wg, wu, wd):
    return (jax.nn.silu(u @ wg) * (u @ wu)) @ wd


def ssd_scan(x, dt, a, bm, cm, h0):
    bsz, n, h, p = x.shape
    g, ns = bm.shape[-2:]
    hg, nc = h // g, n // CHUNK
    xc = x.astype(F32).reshape(bsz, nc, CHUNK, g, hg, p)
    dtc = dt.reshape(bsz, nc, CHUNK, g, hg)
    bc = bm.astype(F32).reshape(bsz, nc, CHUNK, g, ns)
    cc = cm.astype(F32).reshape(bsz, nc, CHUNK, g, ns)
    acum = jnp.cumsum(dtc * a.reshape(g, hg), axis=2)
    seg = acum[:, :, :, None] - acum[:, :, None]
    lower = (jnp.arange(CHUNK)[:, None] >= jnp.arange(CHUNK)[None, :])[:, :, None, None]
    lmat = jnp.exp(jnp.where(lower, seg, -jnp.inf))
    cb = jnp.einsum('bcign,bcjgn->bcijg', cc, bc)
    y_intra = jnp.einsum('bcijg,bcijgh,bcjghp->bcighp', cb, lmat, xc * dtc[..., None])
    w_end = jnp.exp(acum[:, :, -1:] - acum) * dtc
    s_chunk = jnp.einsum('bcqgn,bcqgh,bcqghp->bcghpn', bc, w_end, xc)
    c_dec = jnp.exp(acum[:, :, -1])

    def step(hs, inp):
        s, d = inp
        return hs * d[..., None, None] + s, hs

    h_fin, h_in = lax.scan(step, h0.astype(F32).reshape(bsz, g, hg, p, ns),
                           (jnp.moveaxis(s_chunk, 1, 0), jnp.moveaxis(c_dec, 1, 0)))
    h_in = jnp.moveaxis(h_in, 0, 1)
    y_inter = jnp.einsum('bcqgn,bcghpn,bcqgh->bcqghp', cc, h_in, jnp.exp(acum))
    y = (y_intra + y_inter).reshape(bsz, n, h, p).astype(x.dtype)
    return y, h_fin.reshape(bsz, h, p, ns)


def retention_scan(q, k, v, lg, r0):
    bsz, n, h, dk = q.shape
    dv = v.shape[-1]
    nc = n // CHUNK
    qc = q.astype(F32).reshape(bsz, nc, CHUNK, h, dk)
    kc = k.astype(F32).reshape(bsz, nc, CHUNK, h, dk)
    vc = v.astype(F32).reshape(bsz, nc, CHUNK, h, dv)
    idx = jnp.arange(CHUNK, dtype=F32)
    diff = idx[:, None] - idx[None, :]
    dmat = jnp.exp(jnp.where(diff >= 0, diff[None] * lg[:, None, None], -jnp.inf))
    scores = jnp.einsum('bcihd,bcjhd->bchij', qc, kc) * dmat
    inner = jnp.einsum('bchij,bcjhe->bcihe', scores, vc)
    xi = jnp.exp((idx + 1.0)[None, :] * lg[:, None])
    zeta = jnp.exp((CHUNK - 1.0 - idx)[None, :] * lg[:, None])
    s_chunk = jnp.einsum('bcjhd,hj,bcjhe->bchde', kc, zeta, vc)
    c_dec = jnp.exp(CHUNK * lg)[None, :, None, None]

    def step(r, s):
        return r * c_dec + s, r

    r_fin, r_in = lax.scan(step, r0.astype(F32), jnp.moveaxis(s_chunk, 1, 0))
    cross = jnp.einsum('bcihd,bchde,hi->bcihe', qc, jnp.moveaxis(r_in, 0, 1), xi)
    return (inner + cross).reshape(bsz, n, h, dv).astype(v.dtype), r_fin


def mla_attention(q_nope, q_rope, k_nope, k_rope, v):
    bsz, nq = q_nope.shape[:2]
    nb = nq // Q_BLOCK
    scale = (MLA_NOPE + MLA_ROPE) ** -0.5

    def blocks(a):
        return jnp.moveaxis(a.reshape((bsz, nb, Q_BLOCK) + a.shape[2:]), 1, 0)

    def one(qs):
        qn, qr = qs
        s = (jnp.einsum('bqhd,bkhd->bhqk', qn, k_nope)
             + jnp.einsum('bqhr,bkr->bhqk', qr, k_rope)).astype(F32) * scale
        pr = jax.nn.softmax(s, axis=-1).astype(v.dtype)
        return jnp.einsum('bhqk,bkhe->bqhe', pr, v)

    o = lax.map(one, (blocks(q_nope), blocks(q_rope)))
    return jnp.moveaxis(o, 0, 1).reshape(bsz, nq, MLA_HEADS * MLA_V)


def mixer(u, lw, ctx):
    bsz, n, _ = u.shape
    latent = ctx is not None
    (z, xbc, dt_f, dt_b, dq, dkv, k_rope, rq, rk, rv, rg, gates) = split_cols(u @ lw['w_in'], IN_SPLITS)

    xbc = jax.nn.silu(depthwise_conv(xbc, lw['ssd_conv_w'], lw['ssd_conv_b']))
    xs, bm, cm = split_cols(xbc, (SSD_D_INNER, SSD_GROUPS * SSD_STATE, SSD_GROUPS * SSD_STATE))
    xs = xs.reshape(bsz, n, SSD_HEADS, SSD_HEAD_DIM)
    bm = bm.reshape(bsz, n, SSD_GROUPS, SSD_STATE)
    cm = cm.reshape(bsz, n, SSD_GROUPS, SSD_STATE)
    dt_bias = lw['ssd_dt_bias'].astype(F32)
    a = -jnp.exp(lw['ssd_a_log'].astype(F32))
    dtf = jax.nn.softplus(dt_f.astype(F32) + dt_bias[0])
    dtb = jax.nn.softplus(dt_b.astype(F32) + dt_bias[1])
    h0 = ctx[2] if latent else jnp.zeros((bsz, 2, SSD_HEADS, SSD_HEAD_DIM, SSD_STATE), F32)
    flip = lambda t: jnp.flip(t, axis=1)
    y_f, hf = ssd_scan(xs, dtf, a[0], bm, cm, h0[:, 0])
    y_b, hb = ssd_scan(flip(xs), flip(dtb), a[1], flip(bm), flip(cm), h0[:, 1])
    y = y_f + flip(y_b) + xs * lw['ssd_d'][:, None]
    y = rmsnorm(y.reshape(bsz, n, SSD_D_INNER) * jax.nn.silu(z), lw['ssd_norm_w'])
    ssd_out = y @ lw['w_ssd_out']

    cq = rmsnorm(dq, lw['mla_q_norm_w'])
    qfull = (cq @ lw['mla_w_uq']).reshape(bsz, n, MLA_HEADS, MLA_NOPE + MLA_ROPE)
    q_nope, q_rope = qfull[..., :MLA_NOPE], qfull[..., MLA_NOPE:]
    ckv = rmsnorm(dkv, lw['mla_kv_norm_w'])
    if latent:
        q_rope = axial_rope(q_rope)
        ckv_all = jnp.concatenate([ctx[0], ckv], axis=1)
        kr_all = jnp.concatenate([ctx[1], axial_rope(k_rope)], axis=1)
    else:
        ckv_all, kr_all = ckv, k_rope
    kv = (ckv_all @ lw['mla_w_ukv']).reshape(bsz, ckv_all.shape[1], MLA_HEADS, MLA_NOPE + MLA_V)
    att = mla_attention(q_nope, q_rope, kv[..., :MLA_NOPE], kr_all, kv[..., MLA_NOPE:])
    mla_out = att @ lw['w_mla_out']

    rq = rq.reshape(bsz, n, RET_HEADS, RET_DK)
    rk = rk.reshape(bsz, n, RET_HEADS, RET_DK)
    rv = rv.reshape(bsz, n, RET_HEADS, RET_DV)
    if latent:
        rq, rk = axial_rope(rq), axial_rope(rk)
    rk = rk * (RET_DK ** -0.5)
    lg = jax.nn.log_sigmoid(lw['ret_decay'].astype(F32))
    r0 = ctx[3] if latent else jnp.zeros((bsz, 2, RET_HEADS, RET_DK, RET_DV), F32)
    o_f, rf = retention_scan(rq, rk, rv, lg[0], r0[:, 0])
    o_b, rb = retention_scan(flip(rq), flip(rk), flip(rv), lg[1], r0[:, 1])
    o = head_groupnorm(o_f + flip(o_b), lw['ret_gn_w'])
    ret_out = (jax.nn.silu(rg) * o) @ lw['w_ret_out']

    g_ssd, g_mla, g_ret = jnp.split(gates, N_BRANCH, axis=-1)
    merged = (jax.nn.sigmoid(g_ssd) * ssd_out + jax.nn.sigmoid(g_mla) * mla_out
              + jax.nn.sigmoid(g_ret) * ret_out)
    out = merged @ lw['w_out']
    if latent:
        return out, None
    ssd_state = jnp.stack([hf, hb], axis=1).astype(u.dtype)
    ret_state = jnp.stack([rf, rb], axis=1).astype(u.dtype)
    return out, (ckv, k_rope, ssd_state, ret_state)


def trunk_layer(h, mod, lw, ctx):
    sh1, sc1, g1, sh2, sc2, g2, sh3, sc3, g3 = jnp.split(mod, 9, axis=-1)
    u = rmsnorm(h, lw['norm_w'][0]) * (1.0 + sc1) + sh1
    h = h + 0.5 * g1 * swiglu(u, lw['ffn_w_gate'][0], lw['ffn_w_up'][0], lw['ffn_w_down'][0])
    u = rmsnorm(h, lw['norm_w'][1]) * (1.0 + sc2) + sh2
    m, ctx_out = mixer(u, lw, ctx)
    h = h + g2 * m
    u = rmsnorm(h, lw['norm_w'][2]) * (1.0 + sc3) + sh3
    h = h + 0.5 * g3 * swiglu(u, lw['ffn_w_gate'][1], lw['ffn_w_up'][1], lw['ffn_w_down'][1])
    return h, ctx_out


def setup_inputs(seed: int = 0) -> dict:
    key = jax.random.key(seed)
    ks = iter(jax.random.split(key, 40))
    nrm = lambda shape, scale: jax.random.normal(next(ks), shape, F32) * scale
    gain = lambda shape: 1.0 + 0.02 * jax.random.normal(next(ks), shape, F32)
    D = D_MODEL

    x_prompt = nrm((BATCH, SEQ, D), 1.0)
    x_sample = nrm((DEC_BATCH, DEC_SEQ, D), 1.0)
    c = nrm((DEC_BATCH, D), 1.0)
    cache_mla_ckv = nrm((DEC_BATCH, DEPTH, PAST_LEN, MLA_KV_LORA), 1.0)
    cache_mla_krope = nrm((DEC_BATCH, DEPTH, PAST_LEN, MLA_ROPE), 1.0)
    state_ssd = nrm((DEC_BATCH, DEPTH, 2, SSD_HEADS, SSD_HEAD_DIM, SSD_STATE), 0.5)
    state_ret = nrm((DEC_BATCH, DEPTH, 2, RET_HEADS, RET_DK, RET_DV), 1.0)
    c_ctx = nrm((D,), 1.0)

    w_mod = nrm((DEPTH, D, 9 * D), 0.5 * D ** -0.5)
    b_mod = nrm((DEPTH, 9 * D), 0.02)
    norm_w = gain((DEPTH, 3, D))
    ffn_w_gate = nrm((DEPTH, 2, D, D_FF), D ** -0.5)
    ffn_w_up = nrm((DEPTH, 2, D, D_FF), D ** -0.5)
    ffn_w_down = nrm((DEPTH, 2, D_FF, D), D_FF ** -0.5)
    w_in = nrm((DEPTH, D, IN_COLS), D ** -0.5)

    ssd_conv_w = nrm((DEPTH, SSD_CONV, SSD_CONV_CH), SSD_CONV ** -0.5)
    ssd_conv_b = nrm((DEPTH, SSD_CONV_CH), 0.02)
    dt0 = jnp.exp(jax.random.uniform(next(ks), (DEPTH, 2, SSD_HEADS), F32)
                  * (math.log(0.1) - math.log(0.001)) + math.log(0.001))
    ssd_dt_bias = dt0 + jnp.log(-jnp.expm1(-dt0))
    ssd_a_log = jnp.log(jax.random.uniform(next(ks), (DEPTH, 2, SSD_HEADS), F32, 1.0, 16.0))
    ssd_d = gain((DEPTH, SSD_HEADS))
    ssd_norm_w = gain((DEPTH, SSD_D_INNER))

    mla_q_norm_w = gain((DEPTH, MLA_Q_LORA))
    mla_kv_norm_w = gain((DEPTH, MLA_KV_LORA))
    mla_w_uq = nrm((DEPTH, MLA_Q_LORA, MLA_HEADS * (MLA_NOPE + MLA_ROPE)), MLA_Q_LORA ** -0.5)
    mla_w_ukv = nrm((DEPTH, MLA_KV_LORA, MLA_HEADS * (MLA_NOPE + MLA_V)), MLA_KV_LORA ** -0.5)

    hidx = jnp.arange(RET_HEADS, dtype=F32)
    ret_logit0 = jnp.log(jnp.exp2(5.0 + hidx) - 1.0)
    ret_decay = ret_logit0 + nrm((DEPTH, 2, RET_HEADS), 0.05)
    ret_gn_w = gain((DEPTH, RET_HEADS * RET_DV))

    w_ssd_out = nrm((DEPTH, SSD_D_INNER, D), SSD_D_INNER ** -0.5)
    w_mla_out = nrm((DEPTH, MLA_HEADS * MLA_V, D), (MLA_HEADS * MLA_V) ** -0.5)
    w_ret_out = nrm((DEPTH, RET_HEADS * RET_DV, D), (RET_HEADS * RET_DV) ** -0.5)
    w_out = nrm((DEPTH, D, D), D ** -0.5)
    final_norm_w = gain((D,))

    return {'x_prompt': x_prompt, 'x_sample': x_sample, 'c': c,
            'cache_mla_ckv': cache_mla_ckv, 'cache_mla_krope': cache_mla_krope,
            'state_ssd': state_ssd, 'state_ret': state_ret, 'c_ctx': c_ctx,
            'w_mod': w_mod, 'b_mod': b_mod, 'norm_w': norm_w,
            'ffn_w_gate': ffn_w_gate, 'ffn_w_up': ffn_w_up, 'ffn_w_down': ffn_w_down,
            'w_in': w_in, 'ssd_conv_w': ssd_conv_w, 'ssd_conv_b': ssd_conv_b,
            'ssd_dt_bias': ssd_dt_bias, 'ssd_a_log': ssd_a_log, 'ssd_d': ssd_d, 'ssd_norm_w': ssd_norm_w,
            'mla_q_norm_w': mla_q_norm_w, 'mla_kv_norm_w': mla_kv_norm_w,
            'mla_w_uq': mla_w_uq, 'mla_w_ukv': mla_w_ukv,
            'ret_decay': ret_decay, 'ret_gn_w': ret_gn_w,
            'w_ssd_out': w_ssd_out, 'w_mla_out': w_mla_out, 'w_ret_out': w_ret_out,
            'w_out': w_out, 'final_norm_w': final_norm_w}


def reference(x_prompt, x_sample, c, cache_mla_ckv, cache_mla_krope, state_ssd, state_ret, c_ctx,
              w_mod, b_mod, norm_w, ffn_w_gate, ffn_w_up, ffn_w_down, w_in,
              ssd_conv_w, ssd_conv_b, ssd_dt_bias, ssd_a_log, ssd_d, ssd_norm_w,
              mla_q_norm_w, mla_kv_norm_w, mla_w_uq, mla_w_ukv, ret_decay, ret_gn_w,
              w_ssd_out, w_mla_out, w_ret_out, w_out, final_norm_w):
    h_ctx, h_lat = x_prompt, x_sample
    ckv_l, kr_l, ssd_l, ret_l = [], [], [], []
    for l in range(DEPTH):
        lw = dict(norm_w=norm_w[l], ffn_w_gate=ffn_w_gate[l], ffn_w_up=ffn_w_up[l],
                  ffn_w_down=ffn_w_down[l], w_in=w_in[l],
                  ssd_conv_w=ssd_conv_w[l], ssd_conv_b=ssd_conv_b[l], ssd_dt_bias=ssd_dt_bias[l],
                  ssd_a_log=ssd_a_log[l], ssd_d=ssd_d[l], ssd_norm_w=ssd_norm_w[l],
                  mla_q_norm_w=mla_q_norm_w[l], mla_kv_norm_w=mla_kv_norm_w[l],
                  mla_w_uq=mla_w_uq[l], mla_w_ukv=mla_w_ukv[l],
                  ret_decay=ret_decay[l], ret_gn_w=ret_gn_w[l],
                  w_ssd_out=w_ssd_out[l], w_mla_out=w_mla_out[l], w_ret_out=w_ret_out[l],
                  w_out=w_out[l])
        mod_ctx = (jax.nn.silu(c_ctx) @ w_mod[l] + b_mod[l])[None, None, :]
        h_ctx, (ckv, kr, s_ssd, s_ret) = trunk_layer(h_ctx, mod_ctx, lw, None)
        ckv_l.append(ckv)
        kr_l.append(kr)
        ssd_l.append(s_ssd)
        ret_l.append(s_ret)
        mod_lat = (jax.nn.silu(c) @ w_mod[l] + b_mod[l])[:, None, :]
        ctx = (cache_mla_ckv[:, l], cache_mla_krope[:, l], state_ssd[:, l], state_ret[:, l])
        h_lat, _ = trunk_layer(h_lat, mod_lat, lw, ctx)
    y_prompt = rmsnorm(h_ctx, final_norm_w)
    y_sample = rmsnorm(h_lat, final_norm_w)
    new_cache_mla_ckv = jnp.stack(ckv_l, axis=1)
    new_cache_mla_krope = jnp.stack(kr_l, axis=1)
    new_state_ssd = jnp.stack(ssd_l, axis=1)
    new_state_ret = jnp.stack(ret_l, axis=1)
    return (y_prompt, y_sample, new_cache_mla_ckv, new_cache_mla_krope, new_state_ssd, new_state_ret)
```

```python
import functools

import jax
import jax.numpy as jnp
from jax import lax
from jax.experimental import pallas as pl
from jax.experimental.pallas import tpu as pltpu

F32 = jnp.float32
BF16 = jnp.bfloat16

EPS = 1e-6
ROPE_BASE = 10000.0
GRID_W = 64
CHUNK = 128
LANES = 128
SUBLANES = 8
MIB = 2 ** 20

D_MODEL = 1024
D_FF = 2816
FF_CHUNK = 256
SSD_HEADS = 16
SSD_HEAD_DIM = 64
SSD_D_INNER = SSD_HEADS * SSD_HEAD_DIM
SSD_GROUPS = 2
SSD_STATE = 64
SSD_CONV = 5
SSD_CONV_CH = SSD_D_INNER + 2 * SSD_GROUPS * SSD_STATE
MLA_HEADS = 8
MLA_NOPE = 128
MLA_ROPE = 64
MLA_V = 128
MLA_Q_LORA = 384
MLA_KV_LORA = 256
MLA_QPAD = 256
RET_HEADS = 8
RET_DK = 64
RET_DV = 128
N_BRANCH = 3

_C_Z = 0
_C_XBC = _C_Z + SSD_D_INNER
_C_DT = _C_XBC + SSD_CONV_CH
_C_DQ = _C_DT + 2 * SSD_HEADS
_C_DKV = _C_DQ + MLA_Q_LORA
_C_KR = _C_DKV + MLA_KV_LORA
_C_RQ = _C_KR + MLA_ROPE
_C_RK = _C_RQ + RET_HEADS * RET_DK
_C_RV = _C_RK + RET_HEADS * RET_DK
_C_RG = _C_RV + RET_HEADS * RET_DV
_C_GATES = _C_RG + RET_HEADS * RET_DV
IN_COLS = _C_GATES + N_BRANCH * D_MODEL

_M_DTF = MLA_ROPE
_M_DTB = MLA_ROPE + SSD_HEADS


def _params(sem, vmem_mib):
    return pltpu.CompilerParams(dimension_semantics=sem, vmem_limit_bytes=vmem_mib * MIB)


def _silu(x):
    return x * jax.nn.sigmoid(x)


def _dot(a, b):
    return jnp.dot(a, b, preferred_element_type=F32)


def _dot_nt(a, b):
    return lax.dot_general(a, b, (((1,), (1,)), ((), ())), preferred_element_type=F32)


def _split3(x):
    p1 = x.astype(BF16)
    r1 = x - p1.astype(F32)
    p2 = r1.astype(BF16)
    r2 = r1 - p2.astype(F32)
    return p1, p2, r2.astype(BF16)


def _dot3_rhs(m, x):
    p1, p2, p3 = _split3(x)
    return _dot(m, p1) + _dot(m, p2) + _dot(m, p3)


def _dot3_lhs(x, m):
    p1, p2, p3 = _split3(x)
    return _dot(p1, m) + _dot(p2, m) + _dot(p3, m)


def _modnorm(h, nw, sh, sc):
    var = jnp.mean(h * h, axis=-1, keepdims=True)
    return (h * lax.rsqrt(var + EPS)) * nw * (1.0 + sc) + sh


def _rms(x, w):
    var = jnp.mean(x * x, axis=-1, keepdims=True)
    return (x * lax.rsqrt(var + EPS)) * w


def _rope128(x, cos, sin_a, sin_b):
    return x * cos + pltpu.roll(x, LANES - 16, 1) * sin_a + pltpu.roll(x, 16, 1) * sin_b


def _mod_kernel(c_ref, w_ref, b_ref, o_ref):
    c = c_ref[...]
    s = _silu(c).astype(BF16)
    o_ref[0] = _dot(s, w_ref[0].astype(BF16)) + b_ref[0]


def _mod_call(conds, w_mod, b_mod):
    depth, d, n = w_mod.shape
    rows = conds.shape[0]
    tn = n // 8
    return pl.pallas_call(
        _mod_kernel,
        grid=(depth, n // tn),
        in_specs=[pl.BlockSpec((rows, d), lambda l, j: (0, 0)),
                  pl.BlockSpec((1, d, tn), lambda l, j: (l, 0, j)),
                  pl.BlockSpec((1, 1, tn), lambda l, j: (l, 0, j))],
        out_specs=pl.BlockSpec((1, rows, tn), lambda l, j: (l, 0, j)),
        out_shape=jax.ShapeDtypeStruct((depth, rows, n), F32),
        compiler_params=_params(("parallel", "parallel"), 32),
        name="mod",
    )(conds, w_mod, b_mod.reshape(depth, 1, n))


class _Layout:
    def __init__(self, nb, seq, ndb, dseq, tm):
        assert seq % tm == 0 or tm % seq == 0
        assert (nb * seq) % tm == 0 and dseq % tm == 0
        self.tm = tm
        self.t_ctx = nb * seq
        self.t_lat = ndb * dseq
        self.total = self.t_ctx + self.t_lat
        self.n_ctx_tiles = self.t_ctx // tm
        self.tiles_per_sample = dseq // tm
        self.n_tiles = self.total // tm

    def mod_row(self, i):
        return jnp.where(i < self.n_ctx_tiles, 0, 1 + (i - self.n_ctx_tiles) // self.tiles_per_sample)

    def rope_blk(self, i):
        return jnp.where(i < self.n_ctx_tiles, 0, 1 + (i - self.n_ctx_tiles) % self.tiles_per_sample)


class _ScanOrder:
    def __init__(self, nb, nc_ctx, ndb, nc_lat, reverse):
        self.nb, self.nc_ctx, self.ndb, self.nc_lat, self.reverse = nb, nc_ctx, ndb, nc_lat, reverse
        self.n_ctx = nb * nc_ctx
        self.n_steps = self.n_ctx + ndb * nc_lat

    def is_ctx(self, i):
        return i < self.n_ctx

    def _local(self, i):
        j = jnp.maximum(i - self.n_ctx, 0)
        return i // self.nc_ctx, i % self.nc_ctx, j // self.nc_lat, j % self.nc_lat

    def blk(self, i):
        s_c, c_c, s_l, c_l = self._local(i)
        if self.reverse:
            c_c, c_l = self.nc_ctx - 1 - c_c, self.nc_lat - 1 - c_l
        return jnp.where(self.is_ctx(i), s_c * self.nc_ctx + c_c, self.n_ctx + s_l * self.nc_lat + c_l)

    def first(self, i):
        _, c_c, _, c_l = self._local(i)
        return jnp.where(self.is_ctx(i), c_c == 0, c_l == 0)

    def state_slot(self, i):
        return jnp.where(self.is_ctx(i), i // self.nc_ctx, self.nb)

    def lat_seq(self, i):
        return self._local(i)[2]


def _ffn_kernel(h_ref, mod_ref, nw_ref, wg_ref, wu_ref, wd_ref, *rest, sub, n_chunks, final):
    if final:
        fw_ref, o_ref, u_scr, acc_scr = rest
    else:
        o_ref, u_scr, acc_scr = rest
    sh = mod_ref[0, 3 * sub:3 * sub + 1, :]
    sc = mod_ref[0, 3 * sub + 1:3 * sub + 2, :]
    gate = mod_ref[0, 3 * sub + 2:3 * sub + 3, :]
    u_scr[...] = _modnorm(h_ref[...], nw_ref[...], sh, sc).astype(BF16)
    acc_scr[...] = jnp.zeros_like(acc_scr)

    def body(ci, carry):
        u = u_scr[...]
        a = (_silu(_dot(u, wg_ref[ci])) * _dot(u, wu_ref[ci])).astype(BF16)
        acc_scr[...] += _dot(a, wd_ref[ci])
        return carry

    lax.fori_loop(0, n_chunks, body, 0)
    hn = h_ref[...] + (0.5 * gate) * acc_scr[...]
    if final:
        hn = _rms(hn, fw_ref[...])
    o_ref[...] = hn


def _ffn_call(lay, h, mods_l, nw, wg, wu, wd, l, s, sub, final_w=None):
    tm = lay.tm
    d = h.shape[1]
    n_chunks = wg.shape[2]
    final = final_w is not None
    wspec_in = pl.BlockSpec((None, None, n_chunks, d, FF_CHUNK), lambda i: (l, s, 0, 0, 0))
    wspec_out = pl.BlockSpec((None, None, n_chunks, FF_CHUNK, d), lambda i: (l, s, 0, 0, 0))
    in_specs = [pl.BlockSpec((tm, d), lambda i: (i, 0)),
                pl.BlockSpec((1, 9, d), lambda i: (lay.mod_row(i), 0, 0)),
                pl.BlockSpec((1, d), lambda i: (0, 0)),
                wspec_in, wspec_in, wspec_out]
    args = [h, mods_l, nw, wg, wu, wd]
    if final:
        in_specs.append(pl.BlockSpec((1, d), lambda i: (0, 0)))
        args.append(final_w)
    return pl.pallas_call(
        functools.partial(_ffn_kernel, sub=sub, n_chunks=n_chunks, final=final),
        grid=(lay.n_tiles,),
        in_specs=in_specs,
        out_specs=pl.BlockSpec((tm, d), lambda i: (i, 0)),
        out_shape=jax.ShapeDtypeStruct(h.shape, F32),
        scratch_shapes=[pltpu.VMEM((tm, d), BF16), pltpu.VMEM((tm, d), F32)],
        compiler_params=_params(("parallel",), 56),
        name="ffn",
    )(*args)


def _inproj_a_kernel(h_ref, mod_ref, nw_ref, w_ref, z_ref, xbc_ref, sig_ref):
    sh = mod_ref[0, 3:4, :]
    sc = mod_ref[0, 4:5, :]
    u = _modnorm(h_ref[...], nw_ref[...], sh, sc).astype(BF16)
    z_ref[...] = _silu(_dot(u, w_ref[:, 0:SSD_D_INNER]))
    xbc_ref[...] = _dot(u, w_ref[:, SSD_D_INNER:SSD_D_INNER + SSD_CONV_CH])
    sig_ref[...] = jax.nn.sigmoid(_dot(u, w_ref[:, SSD_D_INNER + SSD_CONV_CH:]))


def _inproj_a_call(lay, h, mods_l, nw, w_a, l):
    tm, d = lay.tm, h.shape[1]
    na = w_a.shape[2]
    t = lay.total
    row = lambda i: (i, 0)
    return pl.pallas_call(
        _inproj_a_kernel,
        grid=(lay.n_tiles,),
        in_specs=[pl.BlockSpec((tm, d), row),
                  pl.BlockSpec((1, 9, d), lambda i: (lay.mod_row(i), 0, 0)),
                  pl.BlockSpec((1, d), lambda i: (0, 0)),
                  pl.BlockSpec((None, d, na), lambda i: (l, 0, 0))],
        out_specs=[pl.BlockSpec((tm, SSD_D_INNER), row),
                   pl.BlockSpec((tm, SSD_CONV_CH), row),
                   pl.BlockSpec((tm, N_BRANCH * D_MODEL), row)],
        out_shape=[jax.ShapeDtypeStruct((t, SSD_D_INNER), F32),
                   jax.ShapeDtypeStruct((t, SSD_CONV_CH), F32),
                   jax.ShapeDtypeStruct((t, N_BRANCH * D_MODEL), F32)],
        compiler_params=_params(("parallel",), 56),
        name="inproj_a",
    )(h, mods_l, nw, w_a)


_B_RQ = 0
_B_RK = _B_RQ + RET_HEADS * RET_DK
_B_RV = _B_RK + RET_HEADS * RET_DK
_B_RG = _B_RV + RET_HEADS * RET_DV
_B_DQ = _B_RG + RET_HEADS * RET_DV
_B_DKV = _B_DQ + MLA_Q_LORA
_B_MISC = _B_DKV + MLA_KV_LORA
_B_COLS = _B_MISC + LANES


def _inproj_b_kernel(h_ref, mod_ref, nw_ref, w_ref, rope_ref, qnw_ref, kvnw_ref, wuq_ref, wukv_ref,
                     mbias_ref, rq_ref, rk_ref, rv_ref, rg_ref, q_ref, kv_ref, ckv_ref, misc_ref,
                     krp_ref):
    sh = mod_ref[0, 3:4, :]
    sc = mod_ref[0, 4:5, :]
    u = _modnorm(h_ref[...], nw_ref[...], sh, sc).astype(BF16)
    cos, sin_a, sin_b = rope_ref[0], rope_ref[1], rope_ref[2]

    rq = _dot(u, w_ref[:, _B_RQ:_B_RK])
    rk = _dot(u, w_ref[:, _B_RK:_B_RV])
    for k in range(RET_HEADS * RET_DK // LANES):
        sl = slice(k * LANES, (k + 1) * LANES)
        rq_ref[:, sl] = _rope128(rq[:, sl], cos, sin_a, sin_b).astype(BF16)
        rk_ref[:, sl] = (_rope128(rk[:, sl], cos, sin_a, sin_b) * (RET_DK ** -0.5)).astype(BF16)
    rv_ref[...] = _dot(u, w_ref[:, _B_RV:_B_RG]).astype(BF16)
    rg_ref[...] = _silu(_dot(u, w_ref[:, _B_RG:_B_DQ]))

    cq = _rms(_dot(u, w_ref[:, _B_DQ:_B_DKV]), qnw_ref[...]).astype(BF16)
    q = _dot(cq, wuq_ref[...])
    for hh in range(MLA_HEADS):
        c0 = hh * MLA_QPAD
        q_ref[:, c0:c0 + MLA_NOPE] = q[:, c0:c0 + MLA_NOPE].astype(BF16)
        q_ref[:, c0 + MLA_NOPE:c0 + MLA_QPAD] = _rope128(
            q[:, c0 + MLA_NOPE:c0 + MLA_QPAD], cos, sin_a, sin_b).astype(BF16)

    ckv = _rms(_dot(u, w_ref[:, _B_DKV:_B_MISC]), kvnw_ref[...])
    ckv_ref[...] = ckv
    kv_ref[...] = _dot(ckv.astype(BF16), wukv_ref[...]).astype(BF16)

    misc = _dot(u, w_ref[:, _B_MISC:_B_COLS])
    lane = lax.broadcasted_iota(jnp.int32, misc.shape, 1)
    is_kr = lane < MLA_ROPE
    kr = _rope128(jnp.where(is_kr, misc, 0.0), cos, sin_a, sin_b)
    xb = misc + mbias_ref[...]
    softplus = jnp.maximum(xb, 0.0) + jnp.log1p(jnp.exp(-jnp.abs(xb)))
    misc_ref[...] = jnp.where(is_kr, kr, softplus)
    krp_ref[...] = jnp.where(is_kr, kr, 0.0).astype(BF16)


def _inproj_b_call(lay, h, mods_l, nw, w_b, rope_tab, qnw, kvnw, wuq, wukv, mbias, l):
    tm, d = lay.tm, h.shape[1]
    t = lay.total
    row = lambda i: (i, 0)
    const2 = lambda i: (0, 0)
    lsel = lambda i: (l, 0, 0)
    nq = MLA_HEADS * MLA_QPAD
    nkv = MLA_HEADS * (MLA_NOPE + MLA_V)
    widths = [(RET_HEADS * RET_DK, BF16), (RET_HEADS * RET_DK, BF16), (RET_HEADS * RET_DV, BF16),
              (RET_HEADS * RET_DV, F32), (nq, BF16), (nkv, BF16), (MLA_KV_LORA, F32),
              (LANES, F32), (LANES, BF16)]
    return pl.pallas_call(
        _inproj_b_kernel,
        grid=(lay.n_tiles,),
        in_specs=[pl.BlockSpec((tm, d), row),
                  pl.BlockSpec((1, 9, d), lambda i: (lay.mod_row(i), 0, 0)),
                  pl.BlockSpec((1, d), const2),
                  pl.BlockSpec((None, d, _B_COLS), lsel),
                  pl.BlockSpec((3, tm, LANES), lambda i: (0, lay.rope_blk(i), 0)),
                  pl.BlockSpec((None, 1, MLA_Q_LORA), lsel),
                  pl.BlockSpec((None, 1, MLA_KV_LORA), lsel),
                  pl.BlockSpec((None, MLA_Q_LORA, nq), lsel),
                  pl.BlockSpec((None, MLA_KV_LORA, nkv), lsel),
                  pl.BlockSpec((None, 1, LANES), lsel)],
        out_specs=[pl.BlockSpec((tm, w), row) for w, _ in widths],
        out_shape=[jax.ShapeDtypeStruct((t, w), dt) for w, dt in widths],
        compiler_params=_params(("parallel",), 56),
        name="inproj_b",
    )(h, mods_l, nw, w_b, rope_tab, qnw, kvnw, wuq, wukv, mbias)


_CONV_PAD = SUBLANES


def _conv_kernel(x_ref, prev_ref, next_ref, w_ref, b_ref, o_ref, pad_scr, *, lay, seq):
    i = pl.program_id(0)
    tm = lay.tm
    j = jnp.maximum(i - lay.n_ctx_tiles, 0)
    is_ctx = i < lay.n_ctx_tiles
    tiles_per_seq = seq // tm
    has_prev = jnp.where(is_ctx, i % tiles_per_seq != 0, j % lay.tiles_per_sample != 0)
    has_next = jnp.where(is_ctx, i % tiles_per_seq != tiles_per_seq - 1,
                         j % lay.tiles_per_sample != lay.tiles_per_sample - 1)
    pad_scr[0:_CONV_PAD, :] = jnp.where(has_prev, prev_ref[...], 0.0)
    pad_scr[_CONV_PAD:_CONV_PAD + tm, :] = x_ref[...]
    pad_scr[_CONV_PAD + tm:2 * _CONV_PAD + tm, :] = jnp.where(has_next, next_ref[...], 0.0)
    acc = jnp.zeros(o_ref.shape, F32) + b_ref[...]
    for k in range(SSD_CONV):
        off = _CONV_PAD + k - SSD_CONV // 2
        acc = acc + pad_scr[off:off + tm, :] * w_ref[k:k + 1, :]
    o_ref[...] = _silu(acc)


def _conv_call(lay, xbc, w, b, l, seq):
    t, ch = xbc.shape
    tm = lay.tm
    assert seq % tm == 0
    r8 = tm // _CONV_PAD
    last8 = t // _CONV_PAD - 1
    return pl.pallas_call(
        functools.partial(_conv_kernel, lay=lay, seq=seq),
        grid=(lay.n_tiles,),
        in_specs=[pl.BlockSpec((tm, ch), lambda i: (i, 0)),
                  pl.BlockSpec((_CONV_PAD, ch), lambda i: (jnp.maximum(i * r8 - 1, 0), 0)),
                  pl.BlockSpec((_CONV_PAD, ch), lambda i: (jnp.minimum((i + 1) * r8, last8), 0)),
                  pl.BlockSpec((None, SSD_CONV, ch), lambda i: (l, 0, 0)),
                  pl.BlockSpec((None, 1, ch), lambda i: (l, 0, 0))],
        out_specs=pl.BlockSpec((tm, ch), lambda i: (i, 0)),
        out_shape=jax.ShapeDtypeStruct((t, ch), F32),
        scratch_shapes=[pltpu.VMEM((tm + 2 * _CONV_PAD, ch), F32)],
        compiler_params=_params(("parallel",), 48),
        name="conv",
    )(xbc, xbc, xbc, w, b)


def _ssd_kernel(*refs, order, epilogue, dt_off):
    it = iter(refs)
    x_ref, misc_ref, alog_ref, h0_ref = next(it), next(it), next(it), next(it)
    if epilogue:
        yf_ref, z_ref, dvec_ref, nw_ref = next(it), next(it), next(it), next(it)
    y_ref, st_ref, s_scr, y_scr = next(it), next(it), next(it), next(it)

    reverse = order.reverse
    i = pl.program_id(0)
    gn = SSD_GROUPS * SSD_STATE
    hp = SSD_D_INNER
    half = hp // SSD_GROUPS

    srow = lax.broadcasted_iota(jnp.int32, (gn, hp), 0)
    slane = lax.broadcasted_iota(jnp.int32, (gn, hp), 1)
    blockmask = (srow // SSD_STATE) == (slane // half)

    first = order.first(i)

    @pl.when(jnp.logical_and(first, order.is_ctx(i)))
    def _():
        s_scr[...] = jnp.zeros_like(s_scr)

    @pl.when(jnp.logical_and(first, jnp.logical_not(order.is_ctx(i))))
    def _():
        h0 = h0_ref[0]
        s_scr[...] = jnp.where(blockmask, jnp.concatenate([h0] * SSD_GROUPS, axis=0), 0.0)

    xs = x_ref[:, 0:SSD_D_INNER]
    bm = x_ref[:, SSD_D_INNER:SSD_D_INNER + gn]
    cm = x_ref[:, SSD_D_INNER + gn:SSD_D_INNER + 2 * gn]

    row = lax.broadcasted_iota(jnp.int32, (CHUNK, CHUNK), 0)
    col = lax.broadcasted_iota(jnp.int32, (CHUNK, CHUNK), 1)
    keep = (row <= col) if reverse else (row >= col)
    tri = jnp.where(keep, 1.0, 0.0).astype(BF16)
    last = 0 if reverse else CHUNK - 1

    lane = col
    lane1 = lax.broadcasted_iota(jnp.int32, (1, LANES), 1)
    is_dt = jnp.logical_and(lane >= dt_off, lane < dt_off + SSD_HEADS)
    is_dt1 = jnp.logical_and(lane1 >= dt_off, lane1 < dt_off + SSD_HEADS)
    dt = jnp.where(is_dt, misc_ref[...], 0.0)
    a = jnp.where(is_dt1, -jnp.exp(alog_ref[...]), 0.0)
    acum = _dot3_rhs(tri, dt * a)
    acum_t = acum.T
    dt_t = dt.T
    acum_last = acum[last:last + 1, :]

    erow = lax.broadcasted_iota(jnp.int32, (CHUNK, hp), 0)
    elane = lax.broadcasted_iota(jnp.int32, (CHUNK, hp), 1)
    expand = jnp.where(erow - dt_off == elane // SSD_HEAD_DIM, 1.0, 0.0).astype(BF16)
    w_end = jnp.where(is_dt, jnp.exp(acum_last - acum) * dt, 0.0)
    w_end_x = _dot3_lhs(w_end, expand)
    eac_x = _dot3_lhs(jnp.where(is_dt, jnp.exp(acum), 0.0), expand)

    s_in = s_scr[...]
    cm_b = cm.astype(BF16)
    bm_b = bm.astype(BF16)
    y_inter = _dot(cm_b, s_in.astype(BF16)) * eac_x

    heads_per_group = SSD_HEADS // SSD_GROUPS
    for g in range(SSD_GROUPS):
        gmask = (lane // SSD_STATE) == g
        cb = _dot_nt(jnp.where(gmask, cm, 0.0).astype(BF16), bm_b)
        for pair in range(heads_per_group // 2):
            k = g * (heads_per_group // 2) + pair
            x_slab = xs[:, k * LANES:(k + 1) * LANES]
            y_slab = y_inter[:, k * LANES:(k + 1) * LANES]
            for hl in range(2):
                r = dt_off + 2 * k + hl
                seg = acum[:, r:r + 1] - acum_t[r:r + 1, :]
                m = cb * jnp.exp(jnp.where(keep, seg, -jnp.inf)) * dt_t[r:r + 1, :]
                x_half = jnp.where((lane // SSD_HEAD_DIM) == hl, x_slab, 0.0)
                y_slab = y_slab + _dot(m.astype(BF16), x_half.astype(BF16))
            y_scr[:, k * LANES:(k + 1) * LANES] = y_slab

    xw = (xs * w_end_x).astype(BF16)
    s_inc = _dot(bm.T.astype(BF16), xw)
    dec_x = eac_x[last:last + 1, :]
    s_new = jnp.where(blockmask, s_in * dec_x + s_inc, 0.0)
    s_scr[...] = s_new
    st_ref[0] = s_new[0:SSD_STATE, :] + s_new[SSD_STATE:2 * SSD_STATE, :]

    if epilogue:
        y = (yf_ref[...] + y_scr[...] + xs * dvec_ref[...]) * z_ref[...]
        y_ref[...] = _rms(y, nw_ref[...]).astype(y_ref.dtype)
    else:
        y_ref[...] = y_scr[...]


def _ssd_call(order, xbc_act, misc, alog_lanes, h0, epi=None, name="ssd"):
    t = xbc_act.shape[0]
    dt_off = _M_DTB if order.reverse else _M_DTF
    rowmap = lambda i: (order.blk(i), 0)
    const2 = lambda i: (0, 0)
    in_specs = [pl.BlockSpec((CHUNK, SSD_CONV_CH), rowmap),
                pl.BlockSpec((CHUNK, LANES), rowmap),
                pl.BlockSpec((1, LANES), const2),
                pl.BlockSpec((1, SSD_STATE, SSD_D_INNER), lambda i: (order.lat_seq(i), 0, 0))]
    args = [xbc_act, misc, alog_lanes, h0]
    if epi is not None:
        yf, zact, dvec, nw = epi
        in_specs += [pl.BlockSpec((CHUNK, SSD_D_INNER), rowmap),
                     pl.BlockSpec((CHUNK, SSD_D_INNER), rowmap),
                     pl.BlockSpec((1, SSD_D_INNER), const2),
                     pl.BlockSpec((1, SSD_D_INNER), const2)]
        args += [yf, zact, dvec, nw]
    return pl.pallas_call(
        functools.partial(_ssd_kernel, order=order, epilogue=epi is not None, dt_off=dt_off),
        grid=(order.n_steps,),
        in_specs=in_specs,
        out_specs=[pl.BlockSpec((CHUNK, SSD_D_INNER), rowmap),
                   pl.BlockSpec((1, SSD_STATE, SSD_D_INNER), lambda i: (order.state_slot(i), 0, 0))],
        out_shape=[jax.ShapeDtypeStruct((t, SSD_D_INNER), BF16 if epi is not None else F32),
                   jax.ShapeDtypeStruct((order.nb + 1, SSD_STATE, SSD_D_INNER), F32)],
        scratch_shapes=[pltpu.VMEM((SSD_GROUPS * SSD_STATE, SSD_D_INNER), F32),
                        pltpu.VMEM((CHUNK, SSD_D_INNER), F32)],
        compiler_params=_params(("arbitrary",), 48),
        name=name,
    )(*args)


def _ret_kernel(*refs, order, epilogue):
    it = iter(refs)
    lg_ref, q_ref, k_ref, v_ref, r0_ref = next(it), next(it), next(it), next(it), next(it)
    if epilogue:
        of_ref, rg_ref, gnw_ref = next(it), next(it), next(it)
    o_ref, st_ref = next(it), next(it)
    r_scr, d_scr, xi_scr, zt_scr, o_scr = next(it), next(it), next(it), next(it), next(it)

    reverse = order.reverse
    i = pl.program_id(0)
    row = lax.broadcasted_iota(jnp.int32, (CHUNK, CHUNK), 0)
    col = lax.broadcasted_iota(jnp.int32, (CHUNK, CHUNK), 1)

    @pl.when(i == 0)
    def _():
        rowf = row.astype(F32)
        colf = col.astype(F32)
        for h in range(RET_HEADS):
            lg = lg_ref[h]
            if reverse:
                d_scr[h] = jnp.exp(jnp.where(col >= row, (colf - rowf) * lg, -jnp.inf))
                xi_scr[h] = jnp.exp((CHUNK - rowf) * lg)
                zt_scr[h] = jnp.exp(colf[0:SUBLANES, :] * lg)
            else:
                d_scr[h] = jnp.exp(jnp.where(row >= col, (rowf - colf) * lg, -jnp.inf))
                xi_scr[h] = jnp.exp((rowf + 1.0) * lg)
                zt_scr[h] = jnp.exp((CHUNK - 1.0 - colf[0:SUBLANES, :]) * lg)

    first = order.first(i)

    @pl.when(jnp.logical_and(first, order.is_ctx(i)))
    def _():
        r_scr[...] = jnp.zeros_like(r_scr)

    @pl.when(jnp.logical_and(first, jnp.logical_not(order.is_ctx(i))))
    def _():
        r_scr[...] = r0_ref[0]

    for pair in range(RET_HEADS // 2):
        q_slab = q_ref[:, pair * LANES:(pair + 1) * LANES]
        k_slab = k_ref[:, pair * LANES:(pair + 1) * LANES]
        k_t = k_slab.astype(F32).T
        r_pair = r_scr[pair * LANES:(pair + 1) * LANES, :]
        r_pair_b = r_pair.astype(BF16)
        for hl in range(2):
            h = 2 * pair + hl
            v_h = v_ref[:, h * RET_DV:(h + 1) * RET_DV]
            qm = jnp.where((col // RET_DK) == hl, q_slab.astype(F32), 0.0).astype(BF16)
            scores = _dot_nt(qm, k_slab) * d_scr[h]
            inner = _dot(scores.astype(BF16), v_h)
            cross = _dot(qm, r_pair_b) * xi_scr[h]
            o_scr[:, h * RET_DV:(h + 1) * RET_DV] = inner + cross
            k_tz = (k_t[hl * RET_DK:(hl + 1) * RET_DK, :] * zt_scr[h, 0:1, :]).astype(BF16)
            dec = jnp.exp(jnp.full((1, RET_DV), float(CHUNK), F32) * lg_ref[h])
            r_new = r_pair[hl * RET_DK:(hl + 1) * RET_DK, :] * dec + _dot(k_tz, v_h)
            r_scr[h * RET_DK:(h + 1) * RET_DK, :] = r_new
            st_ref[0, h * RET_DK:(h + 1) * RET_DK, :] = r_new

    if epilogue:
        for h in range(RET_HEADS):
            sl = slice(h * RET_DV, (h + 1) * RET_DV)
            o = of_ref[:, sl] + o_scr[:, sl]
            mu = jnp.mean(o, axis=-1, keepdims=True)
            var = jnp.mean(jnp.square(o - mu), axis=-1, keepdims=True)
            y = ((o - mu) * lax.rsqrt(var + EPS)) * gnw_ref[:, sl]
            o_ref[:, sl] = (rg_ref[:, sl] * y).astype(o_ref.dtype)
    else:
        o_ref[...] = o_scr[...]


def _ret_call(order, lg, rq, rk, rv, r0, epi=None, name="ret"):
    t = rq.shape[0]
    hk = RET_HEADS * RET_DK
    hv = RET_HEADS * RET_DV
    rowmap = lambda i: (order.blk(i), 0)
    in_specs = [pl.BlockSpec(memory_space=pltpu.SMEM),
                pl.BlockSpec((CHUNK, hk), rowmap),
                pl.BlockSpec((CHUNK, hk), rowmap),
                pl.BlockSpec((CHUNK, hv), rowmap),
                pl.BlockSpec((1, hk, RET_DV), lambda i: (order.lat_seq(i), 0, 0))]
    args = [lg, rq, rk, rv, r0]
    if epi is not None:
        o_f, rgact, gnw = epi
        in_specs += [pl.BlockSpec((CHUNK, hv), rowmap),
                     pl.BlockSpec((CHUNK, hv), rowmap),
                     pl.BlockSpec((1, hv), lambda i: (0, 0))]
        args += [o_f, rgact, gnw]
    return pl.pallas_call(
        functools.partial(_ret_kernel, order=order, epilogue=epi is not None),
        grid=(order.n_steps,),
        in_specs=in_specs,
        out_specs=[pl.BlockSpec((CHUNK, hv), rowmap),
                   pl.BlockSpec((1, hk, RET_DV), lambda i: (order.state_slot(i), 0, 0))],
        out_shape=[jax.ShapeDtypeStruct((t, hv), BF16 if epi is not None else F32),
                   jax.ShapeDtypeStruct((order.nb + 1, hk, RET_DV), F32)],
        scratch_shapes=[pltpu.VMEM((hk, RET_DV), F32),
                        pltpu.VMEM((RET_HEADS, CHUNK, CHUNK), F32),
                        pltpu.VMEM((RET_HEADS, CHUNK, CHUNK), F32),
                        pltpu.VMEM((RET_HEADS, SUBLANES, CHUNK), F32),
                        pltpu.VMEM((CHUNK, hv), F32)],
        compiler_params=_params(("arbitrary",), 48),
        name=name,
    )(*args)


def _attn_body(q_ref, kv_ref, krp_ref, kvc_ref, krpc_ref, o_ref, hps):
    scale = (MLA_NOPE + MLA_ROPE) ** -0.5
    hw = MLA_NOPE + MLA_V
    has_cache = kvc_ref is not None
    for hh in range(hps):
        q = q_ref[:, hh * MLA_QPAD:(hh + 1) * MLA_QPAD]
        kcat = jnp.concatenate([kv_ref[:, hh * hw:hh * hw + MLA_NOPE], krp_ref[...]], axis=1)
        s = _dot_nt(q, kcat) * scale
        m = jnp.max(s, axis=-1, keepdims=True)
        if has_cache:
            kccat = jnp.concatenate([kvc_ref[:, hh * hw:hh * hw + MLA_NOPE], krpc_ref[...]], axis=1)
            sc = _dot_nt(q, kccat) * scale
            m = jnp.maximum(m, jnp.max(sc, axis=-1, keepdims=True))
        p = jnp.exp(s - m)
        den = jnp.sum(p, axis=-1, keepdims=True)
        o = _dot(p.astype(BF16), kv_ref[:, hh * hw + MLA_NOPE:(hh + 1) * hw])
        if has_cache:
            pc = jnp.exp(sc - m)
            den = den + jnp.sum(pc, axis=-1, keepdims=True)
            o = o + _dot(pc.astype(BF16), kvc_ref[:, hh * hw + MLA_NOPE:(hh + 1) * hw])
        o_ref[:, hh * MLA_V:(hh + 1) * MLA_V] = (o / den).astype(o_ref.dtype)


def _attn_ctx_kernel(q_ref, kv_ref, krp_ref, o_ref):
    _attn_body(q_ref, kv_ref, krp_ref, None, None, o_ref, MLA_HEADS)


def _attn_lat_kernel(prev_ref, q_ref, kv_ref, krp_ref, kvc_ref, krpc_ref, o_ref):
    del prev_ref
    _attn_body(q_ref, kv_ref, krp_ref, kvc_ref, krpc_ref, o_ref, 1)


def _attn_ctx_call(q_all, kv, krp, n_seq, seq):
    t = q_all.shape[0]
    nq = MLA_HEADS * MLA_QPAD
    nkv = MLA_HEADS * (MLA_NOPE + MLA_V)
    row = lambda s: (s, 0)
    return pl.pallas_call(
        _attn_ctx_kernel,
        grid=(n_seq,),
        in_specs=[pl.BlockSpec((seq, nq), row), pl.BlockSpec((seq, nkv), row),
                  pl.BlockSpec((seq, LANES), row)],
        out_specs=pl.BlockSpec((seq, MLA_HEADS * MLA_V), row),
        out_shape=jax.ShapeDtypeStruct((t, MLA_HEADS * MLA_V), BF16),
        compiler_params=_params(("parallel",), 48),
        name="attn_ctx",
    )(q_all, kv, krp)


def _attn_lat_call(att_ctx, q_all, kv, krp, kvc, krpc, *, n_samp, dseq, past, t_ctx, tq):
    hw = MLA_NOPE + MLA_V
    nqt = dseq // tq
    q0 = t_ctx // tq
    s0 = t_ctx // dseq
    qmap = lambda b, h, i: (q0 + b * nqt + i, h)
    return pl.pallas_call(
        _attn_lat_kernel,
        grid=(n_samp, MLA_HEADS, nqt),
        in_specs=[pl.BlockSpec(memory_space=pl.ANY),
                  pl.BlockSpec((tq, MLA_QPAD), qmap),
                  pl.BlockSpec((dseq, hw), lambda b, h, i: (s0 + b, h)),
                  pl.BlockSpec((dseq, LANES), lambda b, h, i: (s0 + b, 0)),
                  pl.BlockSpec((past, hw), lambda b, h, i: (b, h)),
                  pl.BlockSpec((past, LANES), lambda b, h, i: (b, 0))],
        out_specs=pl.BlockSpec((tq, MLA_V), qmap),
        out_shape=jax.ShapeDtypeStruct(att_ctx.shape, att_ctx.dtype),
        input_output_aliases={0: 0},
        compiler_params=_params(("parallel", "parallel", "arbitrary"), 56),
        name="attn_lat",
    )(att_ctx, q_all, kv, krp, kvc, krpc)


def _kvc_kernel(ckv_ref, w_ref, o_ref):
    o_ref[0] = _dot(ckv_ref[0], w_ref[0]).astype(o_ref.dtype)


def _kvc_call(ckvc, wukv):
    depth, rows, kvl = ckvc.shape
    n = wukv.shape[2]
    tr = min(rows, 512)
    return pl.pallas_call(
        _kvc_kernel,
        grid=(depth, rows // tr),
        in_specs=[pl.BlockSpec((1, tr, kvl), lambda l, i: (l, i, 0)),
                  pl.BlockSpec((1, kvl, n), lambda l, i: (l, 0, 0))],
        out_specs=pl.BlockSpec((1, tr, n), lambda l, i: (l, i, 0)),
        out_shape=jax.ShapeDtypeStruct((depth, rows, n), BF16),
        compiler_params=_params(("parallel", "parallel"), 32),
        name="kv_cache",
    )(ckvc, wukv)


def _merge_kernel(h_ref, mod_ref, sig_ref, ys_ref, ya_ref, yr_ref, ws_ref, wa_ref, wr_ref, wo_ref,
                  o_ref):
    d = D_MODEL
    merged = (sig_ref[:, 0:d] * _dot(ys_ref[...], ws_ref[...])
              + sig_ref[:, d:2 * d] * _dot(ya_ref[...], wa_ref[...])
              + sig_ref[:, 2 * d:3 * d] * _dot(yr_ref[...], wr_ref[...]))
    out = _dot(merged.astype(BF16), wo_ref[...])
    o_ref[...] = h_ref[...] + mod_ref[0, 5:6, :] * out


def _merge_call(lay, h, mods_l, sig, ys, ya, yr, ws, wa, wr, wo, l):
    tm, d = lay.tm, h.shape[1]
    row = lambda i: (i, 0)
    wspec = pl.BlockSpec((None, d, d), lambda i: (l, 0, 0))
    return pl.pallas_call(
        _merge_kernel,
        grid=(lay.n_tiles,),
        in_specs=[pl.BlockSpec((tm, d), row),
                  pl.BlockSpec((1, 9, d), lambda i: (lay.mod_row(i), 0, 0)),
                  pl.BlockSpec((tm, N_BRANCH * d), row),
                  pl.BlockSpec((tm, d), row), pl.BlockSpec((tm, d), row), pl.BlockSpec((tm, d), row),
                  wspec, wspec, wspec, wspec],
        out_specs=pl.BlockSpec((tm, d), row),
        out_shape=jax.ShapeDtypeStruct(h.shape, F32),
        compiler_params=_params(("parallel",), 48),
        name="merge",
    )(h, mods_l, sig, ys, ya, yr, ws, wa, wr, wo)


def _rope_tables(tm, dseq):
    nf = MLA_ROPE // 4
    t = jnp.arange(dseq)
    pos = jnp.stack([t // GRID_W, t % GRID_W], axis=-1).astype(F32)
    inv = ROPE_BASE ** (-jnp.arange(nf, dtype=F32) / nf)
    ang = pos[:, :, None] * inv
    ang = jnp.broadcast_to(ang[:, :, None, :], (dseq, 2, 2, nf)).reshape(dseq, 4 * nf)
    ang = jnp.concatenate([ang, ang], axis=-1)
    first_half = (jnp.arange(LANES) % (2 * nf)) < nf
    cos = jnp.cos(ang)
    sin = jnp.sin(ang)
    sin_a = jnp.where(first_half, -sin, 0.0)
    sin_b = jnp.where(first_half, 0.0, sin)
    ident = jnp.stack([jnp.ones((tm, LANES), F32), jnp.zeros((tm, LANES), F32),
                       jnp.zeros((tm, LANES), F32)])
    return jnp.concatenate([ident, jnp.stack([cos, sin_a, sin_b])], axis=1)


def kernel(x_prompt, x_sample, c, cache_mla_ckv, cache_mla_krope, state_ssd, state_ret, c_ctx, w_mod, b_mod, norm_w, ffn_w_gate, ffn_w_up, ffn_w_down, w_in, ssd_conv_w, ssd_conv_b, ssd_dt_bias, ssd_a_log, ssd_d, ssd_norm_w, mla_q_norm_w, mla_kv_norm_w, mla_w_uq, mla_w_ukv, ret_decay, ret_gn_w, w_ssd_out, w_mla_out, w_ret_out, w_out, final_norm_w):
    nb, seq, d = x_prompt.shape
    ndb, dseq, _ = x_sample.shape
    depth = w_mod.shape[0]
    past = cache_mla_ckv.shape[2]
    t_ctx = nb * seq
    assert d == D_MODEL and w_in.shape[2] == IN_COLS and seq % CHUNK == 0 and dseq % CHUNK == 0
    assert t_ctx % dseq == 0 and dseq % GRID_W == 0

    lay = _Layout(nb, seq, ndb, dseq, tm=256)

    n_cond = -(-(1 + ndb) // SUBLANES) * SUBLANES
    conds = jnp.zeros((n_cond, d), F32).at[0].set(c_ctx).at[1:1 + ndb].set(c)
    mods = _mod_call(conds, w_mod, b_mod).reshape(depth, n_cond, 9, d)

    n_chunks = D_FF // FF_CHUNK
    wg = ffn_w_gate.astype(BF16).reshape(depth, 2, d, n_chunks, FF_CHUNK).transpose(0, 1, 3, 2, 4)
    wu = ffn_w_up.astype(BF16).reshape(depth, 2, d, n_chunks, FF_CHUNK).transpose(0, 1, 3, 2, 4)
    wd = ffn_w_down.astype(BF16).reshape(depth, 2, n_chunks, FF_CHUNK, d)
    w_in_b = w_in.astype(BF16)
    w_a = jnp.concatenate([w_in_b[:, :, _C_Z:_C_DT], w_in_b[:, :, _C_GATES:]], axis=2)
    w_b = jnp.concatenate([w_in_b[:, :, _C_RQ:_C_GATES], w_in_b[:, :, _C_DQ:_C_KR],
                           w_in_b[:, :, _C_KR:_C_RQ], w_in_b[:, :, _C_DT:_C_DQ],
                           jnp.zeros((depth, d, LANES - MLA_ROPE - 2 * SSD_HEADS), BF16)], axis=2)
    wuq = mla_w_uq.astype(BF16).reshape(depth, MLA_Q_LORA, MLA_HEADS, MLA_NOPE + MLA_ROPE)
    wuq = jnp.pad(wuq, ((0, 0), (0, 0), (0, 0), (0, MLA_QPAD - MLA_NOPE - MLA_ROPE)))
    wuq = wuq.reshape(depth, MLA_Q_LORA, MLA_HEADS * MLA_QPAD)
    wukv = mla_w_ukv.astype(BF16)
    ws, wa, wr, wo = (w.astype(BF16) for w in (w_ssd_out, w_mla_out, w_ret_out, w_out))

    mbias = jnp.zeros((depth, 1, LANES), F32)
    mbias = mbias.at[:, 0, _M_DTF:_M_DTF + SSD_HEADS].set(ssd_dt_bias[:, 0])
    mbias = mbias.at[:, 0, _M_DTB:_M_DTB + SSD_HEADS].set(ssd_dt_bias[:, 1])
    alog = jnp.zeros((depth, 2, 1, LANES), F32)
    alog = alog.at[:, 0, 0, _M_DTF:_M_DTF + SSD_HEADS].set(ssd_a_log[:, 0])
    alog = alog.at[:, 1, 0, _M_DTB:_M_DTB + SSD_HEADS].set(ssd_a_log[:, 1])
    dvec = jnp.repeat(ssd_d, SSD_HEAD_DIM, axis=1).reshape(depth, 1, SSD_D_INNER)
    lg = jax.nn.log_sigmoid(ret_decay.astype(F32))
    rope_tab = _rope_tables(lay.tm, dseq)
    conv_b = ssd_conv_b.reshape(depth, 1, SSD_CONV_CH)
    qnw = mla_q_norm_w.reshape(depth, 1, MLA_Q_LORA)
    kvnw = mla_kv_norm_w.reshape(depth, 1, MLA_KV_LORA)

    ckvc = cache_mla_ckv.transpose(1, 0, 2, 3).reshape(depth, ndb * past, MLA_KV_LORA).astype(BF16)
    kvc_all = _kvc_call(ckvc, wukv)
    krpc_all = jnp.pad(cache_mla_krope.transpose(1, 0, 2, 3).reshape(depth, ndb * past, MLA_ROPE),
                       ((0, 0), (0, 0), (0, LANES - MLA_ROPE))).astype(BF16)
    h0_all = state_ssd.transpose(1, 2, 0, 5, 3, 4).reshape(depth, 2, ndb, SSD_STATE, SSD_D_INNER)
    r0_all = state_ret.transpose(1, 2, 0, 3, 4, 5).reshape(depth, 2, ndb, RET_HEADS * RET_DK, RET_DV)

    h = jnp.concatenate([x_prompt.reshape(t_ctx, d), x_sample.reshape(ndb * dseq, d)], axis=0)

    fwd = _ScanOrder(nb, seq // CHUNK, ndb, dseq // CHUNK, reverse=False)
    bwd = _ScanOrder(nb, seq // CHUNK, ndb, dseq // CHUNK, reverse=True)
    ckv_l, kr_l, ssd_l, ret_l = [], [], [], []
    for l in range(depth):
        mods_l = mods[l]
        nw = norm_w[l].reshape(3, 1, d)
        h = _ffn_call(lay, h, mods_l, nw[0], wg, wu, wd, l, 0, 0)

        zact, xbc, sig = _inproj_a_call(lay, h, mods_l, nw[1], w_a, l)
        rq, rk, rv, rgact, q_all, kv, ckv, misc, krp = _inproj_b_call(
            lay, h, mods_l, nw[1], w_b, rope_tab, qnw, kvnw, wuq, wukv, mbias, l)

        xa = _conv_call(lay, xbc, ssd_conv_w, conv_b, l, seq)
        snw = ssd_norm_w[l].reshape(1, SSD_D_INNER)
        yf, hf = _ssd_call(fwd, xa, misc, alog[l, 0], h0_all[l, 0], name="ssd_f")
        ys, hb = _ssd_call(bwd, xa, misc, alog[l, 1], h0_all[l, 1], epi=(yf, zact, dvec[l], snw),
                           name="ssd_b")

        att = _attn_ctx_call(q_all, kv, krp, nb, seq)
        att = _attn_lat_call(att, q_all, kv, krp, kvc_all[l], krpc_all[l], n_samp=ndb, dseq=dseq,
                             past=past, t_ctx=t_ctx, tq=256)

        gnw = ret_gn_w[l].reshape(1, RET_HEADS * RET_DV)
        of, rf = _ret_call(fwd, lg[l, 0], rq, rk, rv, r0_all[l, 0], name="ret_f")
        yr, rb = _ret_call(bwd, lg[l, 1], rq, rk, rv, r0_all[l, 1], epi=(of, rgact, gnw), name="ret_b")

        h = _merge_call(lay, h, mods_l, sig, ys, att, yr, ws, wa, wr, wo, l)
        final_w = final_norm_w.reshape(1, d) if l == depth - 1 else None
        h = _ffn_call(lay, h, mods_l, nw[2], wg, wu, wd, l, 1, 2, final_w=final_w)

        ckv_l.append(ckv[:t_ctx].reshape(nb, seq, MLA_KV_LORA))
        kr_l.append(misc[:t_ctx, :MLA_ROPE].reshape(nb, seq, MLA_ROPE))
        ssd_l.append(jnp.stack([hf[:nb], hb[:nb]], axis=1))
        ret_l.append(jnp.stack([rf[:nb], rb[:nb]], axis=1))

    y_prompt = h[:t_ctx].reshape(nb, seq, d)
    y_sample = h[t_ctx:].reshape(ndb, dseq, d)
    new_ckv = jnp.stack(ckv_l, axis=1)
    new_kr = jnp.stack(kr_l, axis=1)
    new_ssd = jnp.stack(ssd_l, axis=1).reshape(nb, depth, 2, SSD_STATE, SSD_HEADS, SSD_HEAD_DIM)
    new_ssd = new_ssd.transpose(0, 1, 2, 4, 5, 3)
    new_ret = jnp.stack(ret_l, axis=1).reshape(nb, depth, 2, RET_HEADS, RET_DK, RET_DV)
    return (y_prompt, y_sample, new_ckv, new_kr, new_ssd, new_ret)
```

```python
import functools
import math

import jax
import jax.numpy as jnp
from jax import lax
from jax.experimental import pallas as pl
from jax.experimental.pallas import tpu as pltpu

F32 = jnp.float32
BF16 = jnp.bfloat16

EPS = 1e-6
ROPE_BASE = 10000.0
GRID_W = 64
CHUNK = 128
SCAN_CPS = 2
LANES = 128
SUBLANES = 8
MIB = 2 ** 20

D_MODEL = 1024
D_FF = 2816
FF_CHUNK = 256
SSD_HEADS = 16
SSD_HEAD_DIM = 64
SSD_D_INNER = SSD_HEADS * SSD_HEAD_DIM
SSD_GROUPS = 2
SSD_STATE = 64
SSD_CONV = 5
SSD_CONV_CH = SSD_D_INNER + 2 * SSD_GROUPS * SSD_STATE
MLA_HEADS = 8
MLA_NOPE = 128
MLA_ROPE = 64
MLA_V = 128
MLA_Q_LORA = 384
MLA_KV_LORA = 256
MLA_QPAD = 256
ATT_TQ = 512
ATT_TK = 512
RET_HEADS = 8
RET_DK = 64
RET_DV = 128
N_BRANCH = 3

_C_Z = 0
_C_XBC = _C_Z + SSD_D_INNER
_C_DT = _C_XBC + SSD_CONV_CH
_C_DQ = _C_DT + 2 * SSD_HEADS
_C_DKV = _C_DQ + MLA_Q_LORA
_C_KR = _C_DKV + MLA_KV_LORA
_C_RQ = _C_KR + MLA_ROPE
_C_RK = _C_RQ + RET_HEADS * RET_DK
_C_RV = _C_RK + RET_HEADS * RET_DK
_C_RG = _C_RV + RET_HEADS * RET_DV
_C_GATES = _C_RG + RET_HEADS * RET_DV
IN_COLS = _C_GATES + N_BRANCH * D_MODEL

_M_DTF = MLA_ROPE
_M_DTB = MLA_ROPE + SSD_HEADS

_Q_SCALE = (MLA_NOPE + MLA_ROPE) ** -0.5 * math.log2(math.e)


def _params(sem, vmem_mib):
    return pltpu.CompilerParams(dimension_semantics=sem, vmem_limit_bytes=vmem_mib * MIB)


def _silu(x):
    return x * jax.nn.sigmoid(x)


def _dot(a, b):
    return jnp.dot(a, b, preferred_element_type=F32)


def _dot_nt(a, b):
    return lax.dot_general(a, b, (((1,), (1,)), ((), ())), preferred_element_type=F32)


def _split(x, pieces):
    out = []
    for _ in range(pieces - 1):
        p = x.astype(BF16)
        out.append(p)
        x = x - p.astype(F32)
    out.append(x.astype(BF16))
    return out


def _dot_split_rhs(m, x, pieces):
    return sum(_dot(m, p) for p in _split(x, pieces))


def _dot_split_lhs(x, m, pieces):
    return sum(_dot(p, m) for p in _split(x, pieces))


def _modnorm(h, nw, sh, sc):
    var = jnp.mean(h * h, axis=-1, keepdims=True)
    return (h * lax.rsqrt(var + EPS)) * nw * (1.0 + sc) + sh


def _rms(x, w):
    var = jnp.mean(x * x, axis=-1, keepdims=True)
    return (x * lax.rsqrt(var + EPS)) * w


def _rope128(x, cos, sin_a, sin_b):
    return x * cos + pltpu.roll(x, LANES - 16, 1) * sin_a + pltpu.roll(x, 16, 1) * sin_b


def _mod_kernel(c_ref, w_ref, b_ref, o_ref):
    c = c_ref[...]
    s = _silu(c).astype(BF16)
    o_ref[0] = _dot(s, w_ref[0].astype(BF16)) + b_ref[0]


def _mod_call(conds, w_mod, b_mod):
    depth, d, n = w_mod.shape
    rows = conds.shape[0]
    tn = n // 8
    return pl.pallas_call(
        _mod_kernel,
        grid=(depth, n // tn),
        in_specs=[pl.BlockSpec((rows, d), lambda l, j: (0, 0)),
                  pl.BlockSpec((1, d, tn), lambda l, j: (l, 0, j)),
                  pl.BlockSpec((1, 1, tn), lambda l, j: (l, 0, j))],
        out_specs=pl.BlockSpec((1, rows, tn), lambda l, j: (l, 0, j)),
        out_shape=jax.ShapeDtypeStruct((depth, rows, n), F32),
        compiler_params=_params(("parallel", "parallel"), 32),
        name="mod",
    )(conds, w_mod, b_mod.reshape(depth, 1, n))


class _Layout:
    def __init__(self, nb, seq, ndb, dseq, tm):
        assert seq % tm == 0 or tm % seq == 0
        assert (nb * seq) % tm == 0 and dseq % tm == 0
        self.tm = tm
        self.t_ctx = nb * seq
        self.t_lat = ndb * dseq
        self.total = self.t_ctx + self.t_lat
        self.n_ctx_tiles = self.t_ctx // tm
        self.tiles_per_sample = dseq // tm
        self.n_tiles = self.total // tm

    def mod_row(self, i):
        return jnp.where(i < self.n_ctx_tiles, 0, 1 + (i - self.n_ctx_tiles) // self.tiles_per_sample)

    def rope_blk(self, i):
        return jnp.where(i < self.n_ctx_tiles, 0, 1 + (i - self.n_ctx_tiles) % self.tiles_per_sample)


class _ScanOrder:
    def __init__(self, nb, nc_ctx, ndb, nc_lat, reverse):
        self.nb, self.nc_ctx, self.ndb, self.nc_lat, self.reverse = nb, nc_ctx, ndb, nc_lat, reverse
        self.n_ctx = nb * nc_ctx
        self.n_steps = self.n_ctx + ndb * nc_lat

    def is_ctx(self, i):
        return i < self.n_ctx

    def _local(self, i):
        j = jnp.maximum(i - self.n_ctx, 0)
        return i // self.nc_ctx, i % self.nc_ctx, j // self.nc_lat, j % self.nc_lat

    def blk(self, i):
        s_c, c_c, s_l, c_l = self._local(i)
        if self.reverse:
            c_c, c_l = self.nc_ctx - 1 - c_c, self.nc_lat - 1 - c_l
        return jnp.where(self.is_ctx(i), s_c * self.nc_ctx + c_c, self.n_ctx + s_l * self.nc_lat + c_l)

    def first(self, i):
        _, c_c, _, c_l = self._local(i)
        return jnp.where(self.is_ctx(i), c_c == 0, c_l == 0)

    def state_slot(self, i):
        return jnp.where(self.is_ctx(i), i // self.nc_ctx, self.nb)

    def lat_seq(self, i):
        return self._local(i)[2]

    def chunk_ids(self):
        ids = list(range(SCAN_CPS))
        return ids[::-1] if self.reverse else ids


def _ffn_kernel(h_ref, mod_ref, nw_ref, wg_ref, wu_ref, wd_ref, *rest, sub, final):
    if final:
        fw_ref, o_ref, u_scr, acc_scr = rest
    else:
        o_ref, u_scr, acc_scr = rest
    sh = mod_ref[0, 3 * sub:3 * sub + 1, :]
    sc = mod_ref[0, 3 * sub + 1:3 * sub + 2, :]
    gate = mod_ref[0, 3 * sub + 2:3 * sub + 3, :]
    u_scr[...] = _modnorm(h_ref[...], nw_ref[...], sh, sc).astype(BF16)
    for ci in range(D_FF // FF_CHUNK):
        cols = slice(ci * FF_CHUNK, (ci + 1) * FF_CHUNK)
        u = u_scr[...]
        a = (_silu(_dot(u, wg_ref[:, cols])) * _dot(u, wu_ref[:, cols])).astype(BF16)
        part = _dot(a, wd_ref[cols, :])
        if ci == 0:
            acc_scr[...] = part
        else:
            acc_scr[...] += part
    hn = h_ref[...] + (0.5 * gate) * acc_scr[...]
    if final:
        hn = _rms(hn, fw_ref[...])
    o_ref[...] = hn


def _ffn_call(lay, h, mods_l, nw, wg, wu, wd, l, s, sub, final_w=None):
    tm = lay.tm
    d = h.shape[1]
    final = final_w is not None
    wspec_in = pl.BlockSpec((None, None, d, D_FF), lambda i: (l, s, 0, 0))
    wspec_out = pl.BlockSpec((None, None, D_FF, d), lambda i: (l, s, 0, 0))
    in_specs = [pl.BlockSpec((tm, d), lambda i: (i, 0)),
                pl.BlockSpec((1, 9, d), lambda i: (lay.mod_row(i), 0, 0)),
                pl.BlockSpec((1, d), lambda i: (0, 0)),
                wspec_in, wspec_in, wspec_out]
    args = [h, mods_l, nw, wg, wu, wd]
    if final:
        in_specs.append(pl.BlockSpec((1, d), lambda i: (0, 0)))
        args.append(final_w)
    return pl.pallas_call(
        functools.partial(_ffn_kernel, sub=sub, final=final),
        grid=(lay.n_tiles,),
        in_specs=in_specs,
        out_specs=pl.BlockSpec((tm, d), lambda i: (i, 0)),
        out_shape=jax.ShapeDtypeStruct(h.shape, F32),
        scratch_shapes=[pltpu.VMEM((tm, d), BF16), pltpu.VMEM((tm, d), F32)],
        compiler_params=_params(("parallel",), 56),
        name="ffn",
    )(*args)


def _inproj_a_kernel(h_ref, mod_ref, nw_ref, w_ref, z_ref, xbc_ref, sig_ref):
    sh = mod_ref[0, 3:4, :]
    sc = mod_ref[0, 4:5, :]
    u = _modnorm(h_ref[...], nw_ref[...], sh, sc).astype(BF16)
    z_ref[...] = _silu(_dot(u, w_ref[:, 0:SSD_D_INNER]))
    xbc_ref[...] = _dot(u, w_ref[:, SSD_D_INNER:SSD_D_INNER + SSD_CONV_CH])
    sig_ref[...] = jax.nn.sigmoid(_dot(u, w_ref[:, SSD_D_INNER + SSD_CONV_CH:]))


def _inproj_a_call(lay, h, mods_l, nw, w_a, l):
    tm, d = lay.tm, h.shape[1]
    na = w_a.shape[2]
    t = lay.total
    row = lambda i: (i, 0)
    return pl.pallas_call(
        _inproj_a_kernel,
        grid=(lay.n_tiles,),
        in_specs=[pl.BlockSpec((tm, d), row),
                  pl.BlockSpec((1, 9, d), lambda i: (lay.mod_row(i), 0, 0)),
                  pl.BlockSpec((1, d), lambda i: (0, 0)),
                  pl.BlockSpec((None, d, na), lambda i: (l, 0, 0))],
        out_specs=[pl.BlockSpec((tm, SSD_D_INNER), row),
                   pl.BlockSpec((tm, SSD_CONV_CH), row),
                   pl.BlockSpec((tm, N_BRANCH * D_MODEL), row)],
        out_shape=[jax.ShapeDtypeStruct((t, SSD_D_INNER), F32),
                   jax.ShapeDtypeStruct((t, SSD_CONV_CH), F32),
                   jax.ShapeDtypeStruct((t, N_BRANCH * D_MODEL), F32)],
        compiler_params=_params(("parallel",), 56),
        name="inproj_a",
    )(h, mods_l, nw, w_a)


_B_RQ = 0
_B_RK = _B_RQ + RET_HEADS * RET_DK
_B_RV = _B_RK + RET_HEADS * RET_DK
_B_RG = _B_RV + RET_HEADS * RET_DV
_B_DQ = _B_RG + RET_HEADS * RET_DV
_B_DKV = _B_DQ + MLA_Q_LORA
_B_MISC = _B_DKV + MLA_KV_LORA
_B_COLS = _B_MISC + LANES


def _inproj_b_kernel(h_ref, mod_ref, nw_ref, w_ref, rope_ref, qnw_ref, kvnw_ref, wuq_ref, wukv_ref,
                     mbias_ref, rq_ref, rk_ref, rv_ref, rg_ref, q_ref, kv_ref, ckv_ref, misc_ref,
                     krp_ref):
    sh = mod_ref[0, 3:4, :]
    sc = mod_ref[0, 4:5, :]
    u = _modnorm(h_ref[...], nw_ref[...], sh, sc).astype(BF16)
    cos, sin_a, sin_b = rope_ref[0], rope_ref[1], rope_ref[2]

    rq = _dot(u, w_ref[:, _B_RQ:_B_RK])
    rk = _dot(u, w_ref[:, _B_RK:_B_RV])
    for k in range(RET_HEADS * RET_DK // LANES):
        sl = slice(k * LANES, (k + 1) * LANES)
        rq_ref[:, sl] = _rope128(rq[:, sl], cos, sin_a, sin_b).astype(BF16)
        rk_ref[:, sl] = (_rope128(rk[:, sl], cos, sin_a, sin_b) * (RET_DK ** -0.5)).astype(BF16)
    rv_ref[...] = _dot(u, w_ref[:, _B_RV:_B_RG]).astype(BF16)
    rg_ref[...] = _silu(_dot(u, w_ref[:, _B_RG:_B_DQ]))

    cq = _rms(_dot(u, w_ref[:, _B_DQ:_B_DKV]), qnw_ref[...]).astype(BF16)
    q = _dot(cq, wuq_ref[...]) * _Q_SCALE
    for hh in range(MLA_HEADS):
        c0 = hh * MLA_QPAD
        q_ref[:, c0:c0 + MLA_NOPE] = q[:, c0:c0 + MLA_NOPE].astype(BF16)
        q_ref[:, c0 + MLA_NOPE:c0 + MLA_QPAD] = _rope128(
            q[:, c0 + MLA_NOPE:c0 + MLA_QPAD], cos, sin_a, sin_b).astype(BF16)

    ckv = _rms(_dot(u, w_ref[:, _B_DKV:_B_MISC]), kvnw_ref[...])
    ckv_ref[...] = ckv
    kv_ref[...] = _dot(ckv.astype(BF16), wukv_ref[...]).astype(BF16)

    misc = _dot(u, w_ref[:, _B_MISC:_B_COLS])
    lane = lax.broadcasted_iota(jnp.int32, misc.shape, 1)
    is_kr = lane < MLA_ROPE
    kr = _rope128(jnp.where(is_kr, misc, 0.0), cos, sin_a, sin_b)
    xb = misc + mbias_ref[...]
    softplus = jnp.maximum(xb, 0.0) + jnp.log1p(jnp.exp(-jnp.abs(xb)))
    misc_ref[...] = jnp.where(is_kr, kr, softplus)
    krp_ref[...] = jnp.where(is_kr, kr, 0.0).astype(BF16)


def _inproj_b_call(lay, h, mods_l, nw, w_b, rope_tab, qnw, kvnw, wuq, wukv, mbias, l):
    tm, d = lay.tm, h.shape[1]
    t = lay.total
    row = lambda i: (i, 0)
    const2 = lambda i: (0, 0)
    lsel = lambda i: (l, 0, 0)
    nq = MLA_HEADS * MLA_QPAD
    nkv = MLA_HEADS * (MLA_NOPE + MLA_V)
    widths = [(RET_HEADS * RET_DK, BF16), (RET_HEADS * RET_DK, BF16), (RET_HEADS * RET_DV, BF16),
              (RET_HEADS * RET_DV, F32), (nq, BF16), (nkv, BF16), (MLA_KV_LORA, F32),
              (LANES, F32), (LANES, BF16)]
    return pl.pallas_call(
        _inproj_b_kernel,
        grid=(lay.n_tiles,),
        in_specs=[pl.BlockSpec((tm, d), row),
                  pl.BlockSpec((1, 9, d), lambda i: (lay.mod_row(i), 0, 0)),
                  pl.BlockSpec((1, d), const2),
                  pl.BlockSpec((None, d, _B_COLS), lsel),
                  pl.BlockSpec((3, tm, LANES), lambda i: (0, lay.rope_blk(i), 0)),
                  pl.BlockSpec((None, 1, MLA_Q_LORA), lsel),
                  pl.BlockSpec((None, 1, MLA_KV_LORA), lsel),
                  pl.BlockSpec((None, MLA_Q_LORA, nq), lsel),
                  pl.BlockSpec((None, MLA_KV_LORA, nkv), lsel),
                  pl.BlockSpec((None, 1, LANES), lsel)],
        out_specs=[pl.BlockSpec((tm, w), row) for w, _ in widths],
        out_shape=[jax.ShapeDtypeStruct((t, w), dt) for w, dt in widths],
        compiler_params=_params(("parallel",), 56),
        name="inproj_b",
    )(h, mods_l, nw, w_b, rope_tab, qnw, kvnw, wuq, wukv, mbias)


_CONV_PAD = SUBLANES


def _conv_kernel(x_ref, prev_ref, next_ref, w_ref, b_ref, o_ref, pad_scr, *, lay, seq):
    i = pl.program_id(0)
    tm = lay.tm
    j = jnp.maximum(i - lay.n_ctx_tiles, 0)
    is_ctx = i < lay.n_ctx_tiles
    tiles_per_seq = seq // tm
    has_prev = jnp.where(is_ctx, i % tiles_per_seq != 0, j % lay.tiles_per_sample != 0)
    has_next = jnp.where(is_ctx, i % tiles_per_seq != tiles_per_seq - 1,
                         j % lay.tiles_per_sample != lay.tiles_per_sample - 1)
    pad_scr[0:_CONV_PAD, :] = jnp.where(has_prev, prev_ref[...], 0.0)
    pad_scr[_CONV_PAD:_CONV_PAD + tm, :] = x_ref[...]
    pad_scr[_CONV_PAD + tm:2 * _CONV_PAD + tm, :] = jnp.where(has_next, next_ref[...], 0.0)
    acc = jnp.zeros(o_ref.shape, F32) + b_ref[...]
    for k in range(SSD_CONV):
        off = _CONV_PAD + k - SSD_CONV // 2
        acc = acc + pad_scr[off:off + tm, :] * w_ref[k:k + 1, :]
    o_ref[...] = _silu(acc)


def _conv_call(lay, xbc, w, b, l, seq):
    t, ch = xbc.shape
    tm = lay.tm
    assert seq % tm == 0
    r8 = tm // _CONV_PAD
    last8 = t // _CONV_PAD - 1
    return pl.pallas_call(
        functools.partial(_conv_kernel, lay=lay, seq=seq),
        grid=(lay.n_tiles,),
        in_specs=[pl.BlockSpec((tm, ch), lambda i: (i, 0)),
                  pl.BlockSpec((_CONV_PAD, ch), lambda i: (jnp.maximum(i * r8 - 1, 0), 0)),
                  pl.BlockSpec((_CONV_PAD, ch), lambda i: (jnp.minimum((i + 1) * r8, last8), 0)),
                  pl.BlockSpec((None, SSD_CONV, ch), lambda i: (l, 0, 0)),
                  pl.BlockSpec((None, 1, ch), lambda i: (l, 0, 0))],
        out_specs=pl.BlockSpec((tm, ch), lambda i: (i, 0)),
        out_shape=jax.ShapeDtypeStruct((t, ch), F32),
        scratch_shapes=[pltpu.VMEM((tm + 2 * _CONV_PAD, ch), F32)],
        compiler_params=_params(("parallel",), 48),
        name="conv",
    )(xbc, xbc, xbc, w, b)


def _ssd_chunk(xs, bm, cm, dt_raw, yf_z, consts, s_in, *, reverse, dt_off):
    tri, expand, a, keep, is_dt, lane = consts
    last = 0 if reverse else CHUNK - 1
    gl = SSD_D_INNER // SSD_GROUPS
    heads_per_group = SSD_HEADS // SSD_GROUPS

    dt = jnp.where(is_dt, dt_raw, 0.0)
    acum = _dot_split_rhs(tri, dt * a, 3)
    acum_t = acum.T
    dt_t = dt.T
    w_end = jnp.where(is_dt, jnp.exp2(acum[last:last + 1, :] - acum) * dt, 0.0)
    w_end_x = _dot_split_lhs(w_end, expand, 2)

    cm_b = cm.astype(BF16)
    bm_b = bm.astype(BF16)
    bm_t = bm.T.astype(BF16)
    zeros_s = jnp.zeros((SSD_STATE, gl), BF16)
    lo = lane < SSD_HEAD_DIM

    groups = range(SSD_GROUPS)
    heads = range(SSD_HEADS)
    slabs = range(SSD_HEADS // 2)
    slabs_per_group = heads_per_group // 2
    cbs = [_dot_nt(jnp.where((lane // SSD_STATE) == g, cm, 0.0).astype(BF16), bm_b) for g in groups]
    s_pads = [jnp.concatenate([s_in[0].astype(BF16), zeros_s], axis=0),
              jnp.concatenate([zeros_s, s_in[1].astype(BF16)], axis=0)]
    css = [_dot(cm_b, s_pads[g]) for g in groups]
    cols = [jnp.broadcast_to(acum[:, dt_off + h:dt_off + h + 1], (CHUNK, CHUNK)) for h in heads]
    ms = []
    for h in heads:
        r = dt_off + h
        seg = cols[h] - acum_t[r:r + 1, :]
        ms.append((cbs[h // heads_per_group] * jnp.exp2(jnp.where(keep, seg, -jnp.inf))
                   * dt_t[r:r + 1, :]).astype(BF16))
    x_slabs = [xs[:, k * LANES:(k + 1) * LANES] for k in slabs]
    x_bs = [x.astype(BF16) for x in x_slabs]
    y_intra = [jnp.where(lo, _dot(ms[2 * k], x_bs[k]), _dot(ms[2 * k + 1], x_bs[k])) for k in slabs]
    eacs = [jnp.exp2(jnp.where(lo, cols[2 * k], cols[2 * k + 1])) for k in slabs]
    xws = [(x_slabs[k] * w_end_x[:, k * LANES:(k + 1) * LANES]).astype(BF16) for k in slabs]
    y_slabs, s_out = [], []
    for g in groups:
        ks = range(g * slabs_per_group, (g + 1) * slabs_per_group)
        xw = jnp.concatenate([xws[k] for k in ks], axis=1)
        dec = jnp.concatenate([eacs[k][last:last + 1, :] for k in ks], axis=1)
        s_inc = _dot(bm_t[g * SSD_STATE:(g + 1) * SSD_STATE, :], xw)
        s_out.append(s_in[g] * dec + s_inc)
        for k in ks:
            pair = k - g * slabs_per_group
            y_slabs.append(css[g][:, pair * LANES:(pair + 1) * LANES] * eacs[k] + y_intra[k])
    return jnp.concatenate(y_slabs, axis=1), s_out


def _ssd_kernel(*refs, order, epilogue, dt_off):
    it = iter(refs)
    x_ref, misc_ref, alog_ref, tri_ref, exp_ref, h0_ref = (next(it) for _ in range(6))
    if epilogue:
        yf_ref, z_ref, dvec_ref, nw_ref = next(it), next(it), next(it), next(it)
    y_ref, st_ref, s_scr = next(it), next(it), next(it)

    i = pl.program_id(0)
    gl = SSD_D_INNER // SSD_GROUPS
    first = order.first(i)

    @pl.when(jnp.logical_and(first, order.is_ctx(i)))
    def _():
        s_scr[...] = jnp.zeros_like(s_scr)

    @pl.when(jnp.logical_and(first, jnp.logical_not(order.is_ctx(i))))
    def _():
        s_scr[...] = h0_ref[0]

    row = lax.broadcasted_iota(jnp.int32, (CHUNK, CHUNK), 0)
    lane = lax.broadcasted_iota(jnp.int32, (CHUNK, CHUNK), 1)
    lane1 = lax.broadcasted_iota(jnp.int32, (1, LANES), 1)
    keep = (row <= lane) if order.reverse else (row >= lane)
    is_dt = jnp.logical_and(lane >= dt_off, lane < dt_off + SSD_HEADS)
    is_dt1 = jnp.logical_and(lane1 >= dt_off, lane1 < dt_off + SSD_HEADS)
    a = jnp.where(is_dt1, -jnp.exp(alog_ref[...]) * math.log2(math.e), 0.0)
    consts = (tri_ref[...], exp_ref[...], a, keep, is_dt, lane)

    s_all = s_scr[...]
    state = [s_all[:, g * gl:(g + 1) * gl] for g in range(SSD_GROUPS)]
    y_rows = [None] * SCAN_CPS
    gn = SSD_GROUPS * SSD_STATE
    for ci in order.chunk_ids():
        rows = slice(ci * CHUNK, (ci + 1) * CHUNK)
        xs = x_ref[rows, 0:SSD_D_INNER]
        bm = x_ref[rows, SSD_D_INNER:SSD_D_INNER + gn]
        cm = x_ref[rows, SSD_D_INNER + gn:SSD_D_INNER + 2 * gn]
        y, state = _ssd_chunk(xs, bm, cm, misc_ref[rows, :], None, consts, state,
                              reverse=order.reverse, dt_off=dt_off)
        if epilogue:
            y = (yf_ref[rows, :] + y + xs * dvec_ref[...]) * z_ref[rows, :]
            y = _rms(y, nw_ref[...])
        y_rows[ci] = y.astype(y_ref.dtype)
    y_ref[...] = jnp.concatenate(y_rows, axis=0)
    s_new = jnp.concatenate(state, axis=1)
    s_scr[...] = s_new
    st_ref[0] = s_new


def _ssd_call(order, xbc_act, misc, alog_lanes, tri, expand, h0, epi=None, name="ssd"):
    t = xbc_act.shape[0]
    rows = SCAN_CPS * CHUNK
    dt_off = _M_DTB if order.reverse else _M_DTF
    rowmap = lambda i: (order.blk(i), 0)
    const2 = lambda i: (0, 0)
    in_specs = [pl.BlockSpec((rows, SSD_CONV_CH), rowmap),
                pl.BlockSpec((rows, LANES), rowmap),
                pl.BlockSpec((1, LANES), const2),
                pl.BlockSpec((CHUNK, CHUNK), const2),
                pl.BlockSpec((CHUNK, SSD_D_INNER), const2),
                pl.BlockSpec((1, SSD_STATE, SSD_D_INNER), lambda i: (order.lat_seq(i), 0, 0))]
    args = [xbc_act, misc, alog_lanes, tri, expand, h0]
    if epi is not None:
        yf, zact, dvec, nw = epi
        in_specs += [pl.BlockSpec((rows, SSD_D_INNER), rowmap),
                     pl.BlockSpec((rows, SSD_D_INNER), rowmap),
                     pl.BlockSpec((1, SSD_D_INNER), const2),
                     pl.BlockSpec((1, SSD_D_INNER), const2)]
        args += [yf, zact, dvec, nw]
    return pl.pallas_call(
        functools.partial(_ssd_kernel, order=order, epilogue=epi is not None, dt_off=dt_off),
        grid=(order.n_steps,),
        in_specs=in_specs,
        out_specs=[pl.BlockSpec((rows, SSD_D_INNER), rowmap),
                   pl.BlockSpec((1, SSD_STATE, SSD_D_INNER), lambda i: (order.state_slot(i), 0, 0))],
        out_shape=[jax.ShapeDtypeStruct((t, SSD_D_INNER), BF16 if epi is not None else F32),
                   jax.ShapeDtypeStruct((order.nb + 1, SSD_STATE, SSD_D_INNER), F32)],
        scratch_shapes=[pltpu.VMEM((SSD_STATE, SSD_D_INNER), F32)],
        compiler_params=_params(("arbitrary",), 48),
        name=name,
    )(*args)


def _ret_kernel(*refs, order, epilogue):
    it = iter(refs)
    lg_ref, q_ref, k_ref, v_ref, r0_ref = next(it), next(it), next(it), next(it), next(it)
    if epilogue:
        of_ref, rg_ref, gnw_ref = next(it), next(it), next(it)
    o_ref, st_ref = next(it), next(it)
    r_scr, d_scr, xi_scr, zt_scr = next(it), next(it), next(it), next(it)

    reverse = order.reverse
    i = pl.program_id(0)
    row = lax.broadcasted_iota(jnp.int32, (CHUNK, CHUNK), 0)
    col = lax.broadcasted_iota(jnp.int32, (CHUNK, CHUNK), 1)

    @pl.when(i == 0)
    def _():
        rowf = row.astype(F32)
        colf = col.astype(F32)
        for h in range(RET_HEADS):
            lg = lg_ref[h]
            if reverse:
                d_scr[h] = jnp.exp(jnp.where(col >= row, (colf - rowf) * lg, -jnp.inf))
                xi_scr[h] = jnp.exp((CHUNK - rowf) * lg)
                zt_scr[h] = jnp.exp(colf[0:SUBLANES, :] * lg)
            else:
                d_scr[h] = jnp.exp(jnp.where(row >= col, (rowf - colf) * lg, -jnp.inf))
                xi_scr[h] = jnp.exp((rowf + 1.0) * lg)
                zt_scr[h] = jnp.exp((CHUNK - 1.0 - colf[0:SUBLANES, :]) * lg)

    first = order.first(i)

    @pl.when(jnp.logical_and(first, order.is_ctx(i)))
    def _():
        r_scr[...] = jnp.zeros_like(r_scr)

    @pl.when(jnp.logical_and(first, jnp.logical_not(order.is_ctx(i))))
    def _():
        r_scr[...] = r0_ref[0]

    r_all = r_scr[...]
    pairs = [r_all[p * LANES:(p + 1) * LANES, :] for p in range(RET_HEADS // 2)]
    decs = [jnp.exp(jnp.full((1, RET_DV), float(CHUNK), F32) * lg_ref[h]) for h in range(RET_HEADS)]
    lo = col < RET_DK
    o_rows = [None] * SCAN_CPS
    heads = range(RET_HEADS)
    for ci in order.chunk_ids():
        rows = slice(ci * CHUNK, (ci + 1) * CHUNK)
        k_slabs = [k_ref[rows, p * LANES:(p + 1) * LANES] for p in range(RET_HEADS // 2)]
        v_hs = [v_ref[rows, h * RET_DV:(h + 1) * RET_DV] for h in heads]
        qms = []
        for p in range(RET_HEADS // 2):
            q_slab = q_ref[rows, p * LANES:(p + 1) * LANES].astype(F32)
            qms.append(jnp.where(lo, q_slab, 0.0).astype(BF16))
            qms.append(jnp.where(lo, 0.0, q_slab).astype(BF16))
        r_bs = [pr.astype(BF16) for pr in pairs]
        scores = [_dot_nt(qms[h], k_slabs[h // 2]) for h in heads]
        cross = [_dot(qms[h], r_bs[h // 2]) for h in heads]
        k_ts = [ks.astype(F32).T for ks in k_slabs]
        k_tzs = [(k_ts[h // 2][(h % 2) * RET_DK:(h % 2 + 1) * RET_DK, :] * zt_scr[h, 0:1, :]).astype(BF16)
                 for h in heads]
        incs = [_dot(k_tzs[h], v_hs[h]) for h in heads]
        sds = [(scores[h] * d_scr[h]).astype(BF16) for h in heads]
        inner = [_dot(sds[h], v_hs[h]) for h in heads]
        o_heads = []
        for h in heads:
            sl = slice(h * RET_DV, (h + 1) * RET_DV)
            o_h = inner[h] + cross[h] * xi_scr[h]
            if epilogue:
                o = of_ref[rows, sl] + o_h
                mu = jnp.mean(o, axis=-1, keepdims=True)
                var = jnp.mean(jnp.square(o - mu), axis=-1, keepdims=True)
                o_h = rg_ref[rows, sl] * (((o - mu) * lax.rsqrt(var + EPS)) * gnw_ref[:, sl])
            o_heads.append(o_h.astype(o_ref.dtype))
        pairs = [jnp.concatenate(
            [pairs[p][hl * RET_DK:(hl + 1) * RET_DK, :] * decs[2 * p + hl] + incs[2 * p + hl]
             for hl in range(2)], axis=0) for p in range(RET_HEADS // 2)]
        o_rows[ci] = jnp.concatenate(o_heads, axis=1)
    o_ref[...] = jnp.concatenate(o_rows, axis=0)
    r_new = jnp.concatenate(pairs, axis=0)
    r_scr[...] = r_new
    st_ref[0] = r_new


def _ret_call(order, lg, rq, rk, rv, r0, epi=None, name="ret"):
    t = rq.shape[0]
    rows = SCAN_CPS * CHUNK
    hk = RET_HEADS * RET_DK
    hv = RET_HEADS * RET_DV
    rowmap = lambda i: (order.blk(i), 0)
    in_specs = [pl.BlockSpec(memory_space=pltpu.SMEM),
                pl.BlockSpec((rows, hk), rowmap),
                pl.BlockSpec((rows, hk), rowmap),
                pl.BlockSpec((rows, hv), rowmap),
                pl.BlockSpec((1, hk, RET_DV), lambda i: (order.lat_seq(i), 0, 0))]
    args = [lg, rq, rk, rv, r0]
    if epi is not None:
        o_f, rgact, gnw = epi
        in_specs += [pl.BlockSpec((rows, hv), rowmap),
                     pl.BlockSpec((rows, hv), rowmap),
                     pl.BlockSpec((1, hv), lambda i: (0, 0))]
        args += [o_f, rgact, gnw]
    return pl.pallas_call(
        functools.partial(_ret_kernel, order=order, epilogue=epi is not None),
        grid=(order.n_steps,),
        in_specs=in_specs,
        out_specs=[pl.BlockSpec((rows, hv), rowmap),
                   pl.BlockSpec((1, hk, RET_DV), lambda i: (order.state_slot(i), 0, 0))],
        out_shape=[jax.ShapeDtypeStruct((t, hv), BF16 if epi is not None else F32),
                   jax.ShapeDtypeStruct((order.nb + 1, hk, RET_DV), F32)],
        scratch_shapes=[pltpu.VMEM((hk, RET_DV), F32),
                        pltpu.VMEM((RET_HEADS, CHUNK, CHUNK), F32),
                        pltpu.VMEM((RET_HEADS, CHUNK, CHUNK), F32),
                        pltpu.VMEM((RET_HEADS, SUBLANES, CHUNK), F32)],
        compiler_params=_params(("arbitrary",), 48),
        name=name,
    )(*args)


def _attn_ctx_kernel(q_ref, kv_ref, krp_ref, o_ref):
    hw = MLA_NOPE + MLA_V
    outs = []
    for hh in range(MLA_HEADS):
        q = q_ref[:, hh * MLA_QPAD:(hh + 1) * MLA_QPAD]
        kcat = jnp.concatenate([kv_ref[:, hh * hw:hh * hw + MLA_NOPE], krp_ref[...]], axis=1)
        s = _dot_nt(q, kcat)
        p = jnp.exp2(s - jnp.max(s, axis=-1, keepdims=True))
        den = jnp.sum(p, axis=-1, keepdims=True)
        o = _dot(p.astype(BF16), kv_ref[:, hh * hw + MLA_NOPE:(hh + 1) * hw])
        outs.append((o / den).astype(o_ref.dtype))
    o_ref[...] = jnp.concatenate(outs, axis=1)


def _attn_lat_kernel(prev_ref, q_ref, kv_ref, krp_ref, kvc_ref, krpc_ref, o_ref, *, dseq, past):
    del prev_ref
    q = q_ref[...]
    blocks = [(kvc_ref, krpc_ref, j * ATT_TK) for j in range(past // ATT_TK)]
    blocks += [(kv_ref, krp_ref, j * ATT_TK) for j in range(dseq // ATT_TK)]
    m = l = acc = None
    for kref, rref, r0 in blocks:
        rows = slice(r0, r0 + ATT_TK)
        kcat = jnp.concatenate([kref[rows, 0:MLA_NOPE], rref[rows, :]], axis=1)
        s = _dot_nt(q, kcat)
        s_max = jnp.max(s, axis=-1, keepdims=True)
        if m is None:
            m = s_max
            p = jnp.exp2(s - m)
            l = sum(p[:, c * LANES:(c + 1) * LANES] for c in range(ATT_TK // LANES))
            acc = _dot(p.astype(BF16), kref[rows, MLA_NOPE:MLA_NOPE + MLA_V])
        else:
            m_new = jnp.maximum(m, s_max)
            alpha = jnp.exp2(m - m_new)
            p = jnp.exp2(s - m_new)
            l = alpha * l + sum(p[:, c * LANES:(c + 1) * LANES] for c in range(ATT_TK // LANES))
            acc = alpha * acc + _dot(p.astype(BF16), kref[rows, MLA_NOPE:MLA_NOPE + MLA_V])
            m = m_new
    den = jnp.sum(l, axis=-1, keepdims=True)
    o_ref[...] = (acc / den).astype(o_ref.dtype)


def _attn_ctx_call(q_all, kv, krp, n_seq, seq):
    t = q_all.shape[0]
    nq = MLA_HEADS * MLA_QPAD
    nkv = MLA_HEADS * (MLA_NOPE + MLA_V)
    row = lambda s: (s, 0)
    return pl.pallas_call(
        _attn_ctx_kernel,
        grid=(n_seq,),
        in_specs=[pl.BlockSpec((seq, nq), row), pl.BlockSpec((seq, nkv), row),
                  pl.BlockSpec((seq, LANES), row)],
        out_specs=pl.BlockSpec((seq, MLA_HEADS * MLA_V), row),
        out_shape=jax.ShapeDtypeStruct((t, MLA_HEADS * MLA_V), BF16),
        compiler_params=_params(("parallel",), 48),
        name="attn_ctx",
    )(q_all, kv, krp)


def _attn_lat_call(att_ctx, q_all, kv, krp, kvc, krpc, *, n_samp, dseq, past, t_ctx):
    hw = MLA_NOPE + MLA_V
    tq = min(ATT_TQ, dseq)
    assert dseq % tq == 0 and t_ctx % tq == 0 and dseq % ATT_TK == 0 and past % ATT_TK == 0
    nqt = dseq // tq
    q0 = t_ctx // tq
    s0 = t_ctx // dseq
    qmap = lambda b, h, i: (q0 + b * nqt + i, h)
    return pl.pallas_call(
        functools.partial(_attn_lat_kernel, dseq=dseq, past=past),
        grid=(n_samp, MLA_HEADS, nqt),
        in_specs=[pl.BlockSpec(memory_space=pl.ANY),
                  pl.BlockSpec((tq, MLA_QPAD), qmap),
                  pl.BlockSpec((dseq, hw), lambda b, h, i: (s0 + b, h)),
                  pl.BlockSpec((dseq, LANES), lambda b, h, i: (s0 + b, 0)),
                  pl.BlockSpec((past, hw), lambda b, h, i: (b, h)),
                  pl.BlockSpec((past, LANES), lambda b, h, i: (b, 0))],
        out_specs=pl.BlockSpec((tq, MLA_V), qmap),
        out_shape=jax.ShapeDtypeStruct(att_ctx.shape, att_ctx.dtype),
        input_output_aliases={0: 0},
        compiler_params=_params(("parallel", "parallel", "arbitrary"), 56),
        name="attn_lat",
    )(att_ctx, q_all, kv, krp, kvc, krpc)


def _kvc_kernel(ckv_ref, w_ref, o_ref):
    o_ref[0] = _dot(ckv_ref[0], w_ref[0]).astype(o_ref.dtype)


def _kvc_call(ckvc, wukv):
    depth, rows, kvl = ckvc.shape
    n = wukv.shape[2]
    tr = min(rows, 512)
    return pl.pallas_call(
        _kvc_kernel,
        grid=(depth, rows // tr),
        in_specs=[pl.BlockSpec((1, tr, kvl), lambda l, i: (l, i, 0)),
                  pl.BlockSpec((1, kvl, n), lambda l, i: (l, 0, 0))],
        out_specs=pl.BlockSpec((1, tr, n), lambda l, i: (l, i, 0)),
        out_shape=jax.ShapeDtypeStruct((depth, rows, n), BF16),
        compiler_params=_params(("parallel", "parallel"), 32),
        name="kv_cache",
    )(ckvc, wukv)


def _merge_kernel(h_ref, mod_ref, sig_ref, ys_ref, ya_ref, yr_ref, ws_ref, wa_ref, wr_ref, wo_ref,
                  o_ref):
    d = D_MODEL
    merged = (sig_ref[:, 0:d] * _dot(ys_ref[...], ws_ref[...])
              + sig_ref[:, d:2 * d] * _dot(ya_ref[...], wa_ref[...])
              + sig_ref[:, 2 * d:3 * d] * _dot(yr_ref[...], wr_ref[...]))
    out = _dot(merged.astype(BF16), wo_ref[...])
    o_ref[...] = h_ref[...] + mod_ref[0, 5:6, :] * out


def _merge_call(lay, h, mods_l, sig, ys, ya, yr, ws, wa, wr, wo, l):
    tm, d = lay.tm, h.shape[1]
    row = lambda i: (i, 0)
    wspec = pl.BlockSpec((None, d, d), lambda i: (l, 0, 0))
    return pl.pallas_call(
        _merge_kernel,
        grid=(lay.n_tiles,),
        in_specs=[pl.BlockSpec((tm, d), row),
                  pl.BlockSpec((1, 9, d), lambda i: (lay.mod_row(i), 0, 0)),
                  pl.BlockSpec((tm, N_BRANCH * d), row),
                  pl.BlockSpec((tm, d), row), pl.BlockSpec((tm, d), row), pl.BlockSpec((tm, d), row),
                  wspec, wspec, wspec, wspec],
        out_specs=pl.BlockSpec((tm, d), row),
        out_shape=jax.ShapeDtypeStruct(h.shape, F32),
        compiler_params=_params(("parallel",), 48),
        name="merge",
    )(h, mods_l, sig, ys, ya, yr, ws, wa, wr, wo)


def _rope_tables(tm, dseq):
    nf = MLA_ROPE // 4
    t = jnp.arange(dseq)
    pos = jnp.stack([t // GRID_W, t % GRID_W], axis=-1).astype(F32)
    inv = ROPE_BASE ** (-jnp.arange(nf, dtype=F32) / nf)
    ang = pos[:, :, None] * inv
    ang = jnp.broadcast_to(ang[:, :, None, :], (dseq, 2, 2, nf)).reshape(dseq, 4 * nf)
    ang = jnp.concatenate([ang, ang], axis=-1)
    first_half = (jnp.arange(LANES) % (2 * nf)) < nf
    cos = jnp.cos(ang)
    sin = jnp.sin(ang)
    sin_a = jnp.where(first_half, -sin, 0.0)
    sin_b = jnp.where(first_half, 0.0, sin)
    ident = jnp.stack([jnp.ones((tm, LANES), F32), jnp.zeros((tm, LANES), F32),
                       jnp.zeros((tm, LANES), F32)])
    return jnp.concatenate([ident, jnp.stack([cos, sin_a, sin_b])], axis=1)


def _scan_consts(dt_off, reverse):
    r = jnp.arange(CHUNK)
    keep = (r[:, None] <= r[None, :]) if reverse else (r[:, None] >= r[None, :])
    tri = keep.astype(BF16)
    expand = ((r[:, None] - dt_off) == (jnp.arange(SSD_D_INNER)[None, :] // SSD_HEAD_DIM)).astype(BF16)
    return tri, expand


def kernel(x_prompt, x_sample, c, cache_mla_ckv, cache_mla_krope, state_ssd, state_ret, c_ctx, w_mod, b_mod, norm_w, ffn_w_gate, ffn_w_up, ffn_w_down, w_in, ssd_conv_w, ssd_conv_b, ssd_dt_bias, ssd_a_log, ssd_d, ssd_norm_w, mla_q_norm_w, mla_kv_norm_w, mla_w_uq, mla_w_ukv, ret_decay, ret_gn_w, w_ssd_out, w_mla_out, w_ret_out, w_out, final_norm_w):
    nb, seq, d = x_prompt.shape
    ndb, dseq, _ = x_sample.shape
    depth = w_mod.shape[0]
    past = cache_mla_ckv.shape[2]
    t_ctx = nb * seq
    scan_rows = SCAN_CPS * CHUNK
    assert d == D_MODEL and w_in.shape[2] == IN_COLS and seq % scan_rows == 0 and dseq % scan_rows == 0
    assert t_ctx % dseq == 0 and dseq % GRID_W == 0

    lay = _Layout(nb, seq, ndb, dseq, tm=256)
    lay_big = _Layout(nb, seq, ndb, dseq, tm=512)

    n_cond = -(-(1 + ndb) // SUBLANES) * SUBLANES
    conds = jnp.zeros((n_cond, d), F32).at[0].set(c_ctx).at[1:1 + ndb].set(c)
    mods = _mod_call(conds, w_mod, b_mod).reshape(depth, n_cond, 9, d)

    wg = ffn_w_gate.astype(BF16)
    wu = ffn_w_up.astype(BF16)
    wd = ffn_w_down.astype(BF16)
    w_in_b = w_in.astype(BF16)
    w_a = jnp.concatenate([w_in_b[:, :, _C_Z:_C_DT], w_in_b[:, :, _C_GATES:]], axis=2)
    w_b = jnp.concatenate([w_in_b[:, :, _C_RQ:_C_GATES], w_in_b[:, :, _C_DQ:_C_KR],
                           w_in_b[:, :, _C_KR:_C_RQ], w_in_b[:, :, _C_DT:_C_DQ],
                           jnp.zeros((depth, d, LANES - MLA_ROPE - 2 * SSD_HEADS), BF16)], axis=2)
    wuq = mla_w_uq.astype(BF16).reshape(depth, MLA_Q_LORA, MLA_HEADS, MLA_NOPE + MLA_ROPE)
    wuq = jnp.pad(wuq, ((0, 0), (0, 0), (0, 0), (0, MLA_QPAD - MLA_NOPE - MLA_ROPE)))
    wuq = wuq.reshape(depth, MLA_Q_LORA, MLA_HEADS * MLA_QPAD)
    wukv = mla_w_ukv.astype(BF16)
    ws, wa, wr, wo = (w.astype(BF16) for w in (w_ssd_out, w_mla_out, w_ret_out, w_out))

    mbias = jnp.zeros((depth, 1, LANES), F32)
    mbias = mbias.at[:, 0, _M_DTF:_M_DTF + SSD_HEADS].set(ssd_dt_bias[:, 0])
    mbias = mbias.at[:, 0, _M_DTB:_M_DTB + SSD_HEADS].set(ssd_dt_bias[:, 1])
    alog = jnp.zeros((depth, 2, 1, LANES), F32)
    alog = alog.at[:, 0, 0, _M_DTF:_M_DTF + SSD_HEADS].set(ssd_a_log[:, 0])
    alog = alog.at[:, 1, 0, _M_DTB:_M_DTB + SSD_HEADS].set(ssd_a_log[:, 1])
    dvec = jnp.repeat(ssd_d, SSD_HEAD_DIM, axis=1).reshape(depth, 1, SSD_D_INNER)
    lg = jax.nn.log_sigmoid(ret_decay.astype(F32))
    rope_tab = _rope_tables(lay.tm, dseq)
    tri_f, expand_f = _scan_consts(_M_DTF, False)
    tri_b, expand_b = _scan_consts(_M_DTB, True)
    conv_b = ssd_conv_b.reshape(depth, 1, SSD_CONV_CH)
    qnw = mla_q_norm_w.reshape(depth, 1, MLA_Q_LORA)
    kvnw = mla_kv_norm_w.reshape(depth, 1, MLA_KV_LORA)

    ckvc = cache_mla_ckv.transpose(1, 0, 2, 3).reshape(depth, ndb * past, MLA_KV_LORA).astype(BF16)
    kvc_all = _kvc_call(ckvc, wukv)
    krpc_all = jnp.pad(cache_mla_krope.transpose(1, 0, 2, 3).reshape(depth, ndb * past, MLA_ROPE),
                       ((0, 0), (0, 0), (0, LANES - MLA_ROPE))).astype(BF16)
    h0_all = state_ssd.transpose(1, 2, 0, 5, 3, 4).reshape(depth, 2, ndb, SSD_STATE, SSD_D_INNER)
    r0_all = state_ret.transpose(1, 2, 0, 3, 4, 5).reshape(depth, 2, ndb, RET_HEADS * RET_DK, RET_DV)

    h = jnp.concatenate([x_prompt.reshape(t_ctx, d), x_sample.reshape(ndb * dseq, d)], axis=0)

    fwd = _ScanOrder(nb, seq // scan_rows, ndb, dseq // scan_rows, reverse=False)
    bwd = _ScanOrder(nb, seq // scan_rows, ndb, dseq // scan_rows, reverse=True)
    ckv_l, kr_l, ssd_l, ret_l = [], [], [], []
    for l in range(depth):
        mods_l = mods[l]
        nw = norm_w[l].reshape(3, 1, d)
        h = _ffn_call(lay_big, h, mods_l, nw[0], wg, wu, wd, l, 0, 0)

        zact, xbc, sig = _inproj_a_call(lay, h, mods_l, nw[1], w_a, l)
        rq, rk, rv, rgact, q_all, kv, ckv, misc, krp = _inproj_b_call(
            lay, h, mods_l, nw[1], w_b, rope_tab, qnw, kvnw, wuq, wukv, mbias, l)

        xa = _conv_call(lay, xbc, ssd_conv_w, conv_b, l, seq)
        snw = ssd_norm_w[l].reshape(1, SSD_D_INNER)
        yf, hf = _ssd_call(fwd, xa, misc, alog[l, 0], tri_f, expand_f, h0_all[l, 0], name="ssd_f")
        ys, hb = _ssd_call(bwd, xa, misc, alog[l, 1], tri_b, expand_b, h0_all[l, 1],
                           epi=(yf, zact, dvec[l], snw), name="ssd_b")

        att = _attn_ctx_call(q_all, kv, krp, nb, seq)
        att = _attn_lat_call(att, q_all, kv, krp, kvc_all[l], krpc_all[l], n_samp=ndb, dseq=dseq,
                             past=past, t_ctx=t_ctx)

        gnw = ret_gn_w[l].reshape(1, RET_HEADS * RET_DV)
        of, rf = _ret_call(fwd, lg[l, 0], rq, rk, rv, r0_all[l, 0], name="ret_f")
        yr, rb = _ret_call(bwd, lg[l, 1], rq, rk, rv, r0_all[l, 1], epi=(of, rgact, gnw), name="ret_b")

        h = _merge_call(lay, h, mods_l, sig, ys, att, yr, ws, wa, wr, wo, l)
        final_w = final_norm_w.reshape(1, d) if l == depth - 1 else None
        h = _ffn_call(lay_big, h, mods_l, nw[2], wg, wu, wd, l, 1, 2, final_w=final_w)

        ckv_l.append(ckv[:t_ctx].reshape(nb, seq, MLA_KV_LORA))
        kr_l.append(misc[:t_ctx, :MLA_ROPE].reshape(nb, seq, MLA_ROPE))
        ssd_l.append(jnp.stack([hf[:nb], hb[:nb]], axis=1))
        ret_l.append(jnp.stack([rf[:nb], rb[:nb]], axis=1))

    y_prompt = h[:t_ctx].reshape(nb, seq, d)
    y_sample = h[t_ctx:].reshape(ndb, dseq, d)
    new_ckv = jnp.stack(ckv_l, axis=1)
    new_kr = jnp.stack(kr_l, axis=1)
    new_ssd = jnp.stack(ssd_l, axis=1).reshape(nb, depth, 2, SSD_STATE, SSD_HEADS, SSD_HEAD_DIM)
    new_ssd = new_ssd.transpose(0, 1, 2, 4, 5, 3)
    new_ret = jnp.stack(ret_l, axis=1).reshape(nb, depth, 2, RET_HEADS, RET_DK, RET_DV)
    return (y_prompt, y_sample, new_ckv, new_kr, new_ssd, new_ret)
```
